```python
import jax
import jax.numpy as jnp
from jax import lax
import numpy as np

D_MODEL = 1024
BATCH = 8
SEQ = 8192
DEPTH = 2

GLA_HEADS = 4
GLA_DK = 64
GLA_DV = 128
GLA_GATE_RANK = 16
GLA_GATE_TEMP = 16.0
GLA_CHUNK = 64
NSA_HEADS = 8
NSA_KV_GROUPS = 2
NSA_HPG = NSA_HEADS // NSA_KV_GROUPS
NSA_DH = 64
NSA_CMP_BLOCK = 32
NSA_CMP_STRIDE = 16
NSA_CMP_HIDDEN = 128
NSA_SEL_BLOCK = 64
NSA_TOPN = 16
NSA_WINDOW = 512
NSA_QBLOCK = 128
NSA_BRANCHES = 3
SGU_WIDTH = 2 * D_MODEL
SGU_GROUPS = 8
SGU_CHUNK = 128
FFN_HIDDEN = 4 * D_MODEL
NORM_EPS = 1e-6
NEG_BIG = -1e30
POS_BIG = 1e30

EVEN_PROJ_SIZES = (GLA_HEADS * GLA_DK, GLA_HEADS * GLA_DK, GLA_HEADS * GLA_DV, GLA_GATE_RANK, GLA_HEADS * GLA_DV, NSA_HEADS * NSA_DH, NSA_KV_GROUPS * NSA_DH, NSA_KV_GROUPS * NSA_DH, NSA_KV_GROUPS * NSA_DH, NSA_KV_GROUPS * NSA_DH, NSA_KV_GROUPS * NSA_DH, NSA_KV_GROUPS * NSA_DH, NSA_HEADS * NSA_BRANCHES)
EVEN_PROJ_WIDTH = sum(EVEN_PROJ_SIZES)
EVEN_MIX_WIDTH = GLA_HEADS * GLA_DV + NSA_HEADS * NSA_DH

kernel_name = 'hybrid_gla_nsa_sgu_trunk'


def _rms_norm(x, g):
    xf = x.astype(jnp.float32)
    y = xf * lax.rsqrt(jnp.mean(xf * xf, axis=-1, keepdims=True) + NORM_EPS)
    return (y * g).astype(x.dtype)


def _masked_softmax(s, mask):
    m = mask.astype(jnp.float32)
    s = jnp.where(mask, s, NEG_BIG)
    p = jnp.exp(s - jnp.max(s, axis=-1, keepdims=True)) * m
    return p / jnp.maximum(jnp.sum(p, axis=-1, keepdims=True), 1e-30)


def _squared_relu_mlp(x, w1, w2):
    h = jax.nn.relu(x @ w1)
    return (h * h) @ w2


def _gla(q, k, v, g_lr, r, w_gate, b_gate, norm_g):
    f32 = jnp.float32
    bsz, seq, _ = q.shape
    c = GLA_CHUNK
    nc = seq // c
    glog = jax.nn.log_sigmoid((g_lr @ w_gate + b_gate).astype(f32)) / GLA_GATE_TEMP

    def chunks(t, d):
        return t.astype(f32).reshape(bsz, nc, c, GLA_HEADS, d).transpose(0, 3, 1, 2, 4)

    qc = chunks(q, GLA_DK) * (GLA_DK ** -0.5)
    kc = chunks(k, GLA_DK)
    vc = chunks(v, GLA_DV)
    b = jnp.cumsum(chunks(glog, GLA_DK), axis=3)
    b_last = b[:, :, :, -1:, :]
    q_t = qc * jnp.exp(b)
    k_t = kc * jnp.exp(-b)
    k_end = kc * jnp.exp(b_last - b)
    causal = jnp.tril(jnp.ones((c, c), dtype=bool))
    a = jnp.where(causal, jnp.einsum('bhnid,bhnjd->bhnij', q_t, k_t), 0.0)
    o_intra = jnp.einsum('bhnij,bhnjv->bhniv', a, vc)
    ds = jnp.einsum('bhnjd,bhnjv->bhndv', k_end, vc)
    decay = jnp.exp(b_last[:, :, :, 0, :])

    def step(state, inp):
        dec, d = inp
        return dec[..., None] * state + d, state

    s0 = jnp.zeros((bsz, GLA_HEADS, GLA_DK, GLA_DV), f32)
    _, s_prev = lax.scan(step, s0, (jnp.moveaxis(decay, 2, 0), jnp.moveaxis(ds, 2, 0)))
    s_prev = jnp.moveaxis(s_prev, 0, 2)
    o = o_intra + jnp.einsum('bhnid,bhndv->bhniv', q_t, s_prev)
    o = o.transpose(0, 2, 3, 1, 4).reshape(bsz, seq, GLA_HEADS, GLA_DV)
    o = o * lax.rsqrt(jnp.mean(o * o, axis=-1, keepdims=True) + NORM_EPS) * norm_g
    o = o.reshape(bsz, seq, GLA_HEADS * GLA_DV) * jax.nn.silu(r.astype(f32))
    return o.astype(q.dtype)


def _nsa(q, kc, vc, ks, vs, kw, vw, gate_logits, cmp_pos, cmp_w1, cmp_w2):
    f32 = jnp.float32
    bsz, seq, _ = q.shape
    g_n, hpg, dh, qb_len, w = NSA_KV_GROUPS, NSA_HPG, NSA_DH, NSA_QBLOCK, NSA_WINDOW
    n_qb = seq // qb_len
    n_cmp = (seq - NSA_CMP_BLOCK) // NSA_CMP_STRIDE + 1
    n_sel = seq // NSA_SEL_BLOCK
    topn = min(NSA_TOPN, n_sel)
    scale = dh ** -0.5

    def kv_heads(t):
        return t.reshape(bsz, seq, g_n, dh)

    cmp_idx = jnp.arange(n_cmp)[:, None] * NSA_CMP_STRIDE + jnp.arange(NSA_CMP_BLOCK)[None, :]

    def compress(t, i):
        blocks = kv_heads(t)[:, cmp_idx] + cmp_pos[i][None, None, :, None, :]
        blocks = blocks.transpose(0, 3, 1, 2, 4).reshape(bsz, g_n, n_cmp, NSA_CMP_BLOCK * dh)
        return jax.nn.gelu(blocks @ cmp_w1[i]) @ cmp_w2[i]

    k_cmp = compress(kc, 0)
    v_cmp = compress(vc, 1)
    cmp_start = jnp.arange(n_cmp) * NSA_CMP_STRIDE
    cmp_end = cmp_start + NSA_CMP_BLOCK - 1
    sel_start = jnp.arange(n_sel) * NSA_SEL_BLOCK
    overlap = ((cmp_start[:, None] < sel_start[None, :] + NSA_SEL_BLOCK)
               & (cmp_start[:, None] + NSA_CMP_BLOCK > sel_start[None, :])).astype(f32)

    k_sel = kv_heads(ks).reshape(bsz, n_sel, NSA_SEL_BLOCK, g_n, dh).transpose(0, 3, 1, 2, 4)
    v_sel = kv_heads(vs).reshape(bsz, n_sel, NSA_SEL_BLOCK, g_n, dh).transpose(0, 3, 1, 2, 4)
    pad = ((0, 0), (0, 0), (w, 0), (0, 0))
    k_win = jnp.pad(kv_heads(kw).transpose(0, 2, 1, 3), pad)
    v_win = jnp.pad(kv_heads(vw).transpose(0, 2, 1, 3), pad)

    qb = (q * scale).reshape(bsz, n_qb, qb_len, g_n, hpg, dh).transpose(1, 0, 3, 4, 2, 5)
    gb = jax.nn.sigmoid(gate_logits.astype(f32)).reshape(bsz, n_qb, qb_len, g_n, hpg, NSA_BRANCHES)
    gb = gb.transpose(1, 0, 3, 4, 2, 5)
    b_ix = jnp.arange(bsz)[:, None, None, None]
    g_ix = jnp.arange(g_n)[None, :, None, None]
    within = jnp.arange(NSA_SEL_BLOCK)
    win_off = jnp.arange(qb_len + w)
    sel_j = jnp.arange(n_sel)

    def block(args):
        n, qn, gn = args
        t = n * qb_len + jnp.arange(qb_len)
        s_c = jnp.einsum('bghqd,bgcd->bghqc', qn, k_cmp).astype(f32)
        p_c = _masked_softmax(s_c, cmp_end[None, :] <= t[:, None])
        o_c = jnp.einsum('bghqc,bgcd->bghqd', p_c.astype(v_cmp.dtype), v_cmp)
        imp = jnp.einsum('bgqc,cj->bgqj', jnp.sum(p_c, axis=2), overlap)
        cur = t // NSA_SEL_BLOCK
        forced = (sel_j[None, :] == 0) | (sel_j[None, :] == cur[:, None]) | (sel_j[None, :] == cur[:, None] - 1)
        allowed = sel_start[None, :] <= t[:, None]
        score = jnp.where(forced, POS_BIG, jnp.where(allowed, imp, NEG_BIG))
        top_val, top_idx = lax.top_k(score, topn)
        blk_ok = top_val > 0.5 * NEG_BIG
        k_g = k_sel[b_ix, g_ix, top_idx]
        v_g = v_sel[b_ix, g_ix, top_idx]
        kpos = top_idx[..., None] * NSA_SEL_BLOCK + within
        ok = blk_ok[..., None] & (kpos <= t[None, None, :, None, None])
        s_s = jnp.einsum('bghqd,bgqnkd->bghqnk', qn, k_g).astype(f32)
        s_s = s_s.reshape(bsz, g_n, hpg, qb_len, topn * NSA_SEL_BLOCK)
        p_s = _masked_softmax(s_s, ok.reshape(bsz, g_n, 1, qb_len, topn * NSA_SEL_BLOCK))
        p_s = p_s.reshape(bsz, g_n, hpg, qb_len, topn, NSA_SEL_BLOCK).astype(v_g.dtype)
        o_s = jnp.einsum('bghqnk,bgqnkd->bghqd', p_s, v_g)
        k_w = lax.dynamic_slice_in_dim(k_win, n * qb_len, qb_len + w, axis=2)
        v_w = lax.dynamic_slice_in_dim(v_win, n * qb_len, qb_len + w, axis=2)
        kpos_w = n * qb_len - w + win_off
        ok_w = (kpos_w[None, :] >= 0) & (kpos_w[None, :] <= t[:, None]) & (kpos_w[None, :] > t[:, None] - w)
        s_w = jnp.einsum('bghqd,bgkd->bghqk', qn, k_w).astype(f32)
        p_w = _masked_softmax(s_w, ok_w)
        o_w = jnp.einsum('bghqk,bgkd->bghqd', p_w.astype(v_w.dtype), v_w)
        out = gn[..., 0:1] * o_c + gn[..., 1:2] * o_s + gn[..., 2:3] * o_w
        return out.astype(q.dtype)

    o = lax.map(block, (jnp.arange(n_qb), qb, gb))
    return o.transpose(1, 0, 4, 2, 3, 5).reshape(bsz, seq, NSA_HEADS * dh)


def _even_mixer(x, w_in, w_out, gla_w_gate, gla_b_gate, gla_norm, nsa_gate_b, nsa_cmp_pos, nsa_cmp_w1, nsa_cmp_w2):
    offsets = [int(o) for o in np.cumsum(EVEN_PROJ_SIZES)[:-1]]
    gq, gk, gv, glr, gr, nq, kc, vc, ks, vs, kw, vw, ng = jnp.split(x @ w_in, offsets, axis=-1)
    o_a = _gla(gq, gk, gv, glr, gr, gla_w_gate, gla_b_gate, gla_norm)
    o_b = _nsa(nq, kc, vc, ks, vs, kw, vw, ng + nsa_gate_b, nsa_cmp_pos, nsa_cmp_w1, nsa_cmp_w2)
    return jnp.concatenate([o_a, o_b], axis=-1) @ w_out


def _sgu_mixer(x, w_in, ln_g, ln_b, w_s, b_s, w_out):
    bsz, seq, _ = x.shape
    h = jax.nn.gelu(x @ w_in)
    u, v = jnp.split(h, 2, axis=-1)
    vf = v.astype(jnp.float32)
    mu = jnp.mean(vf, axis=-1, keepdims=True)
    var = jnp.mean((vf - mu) ** 2, axis=-1, keepdims=True)
    v = ((vf - mu) * lax.rsqrt(var + NORM_EPS) * ln_g + ln_b).astype(x.dtype)
    n_ch = seq // SGU_CHUNK
    v = v.reshape(bsz, n_ch, SGU_CHUNK, SGU_GROUPS, SGU_WIDTH // SGU_GROUPS)
    w_causal = jnp.where(jnp.tril(jnp.ones((SGU_CHUNK, SGU_CHUNK), dtype=bool)), w_s, 0.0)
    mixed = jnp.einsum('gts,bnsgc->bntgc', w_causal, v) + b_s.T[:, :, None]
    y = u * mixed.reshape(bsz, seq, SGU_WIDTH)
    return y @ w_out


def setup_inputs(seed: int = 0) -> dict:
    key = jax.random.key(seed)
    k = jax.random.split(key, 20)
    ne = (DEPTH + 1) // 2
    no = DEPTH // 2
    f32 = jnp.float32

    def normal(kk, shape):
        return jax.random.normal(kk, shape, f32)

    def dense(kk, shape, fan_in):
        return normal(kk, shape) * (fan_in ** -0.5)

    return {
        'x': normal(k[0], (BATCH, SEQ, D_MODEL)),
        'norm_g': 1.0 + 0.05 * normal(k[1], (DEPTH, 4, D_MODEL)),
        'ffn_w1': dense(k[2], (DEPTH, D_MODEL, FFN_HIDDEN), D_MODEL),
        'ffn_w2': dense(k[3], (DEPTH, FFN_HIDDEN, D_MODEL), FFN_HIDDEN),
        'e_w_in': dense(k[4], (ne, D_MODEL, EVEN_PROJ_WIDTH), D_MODEL),
        'e_w_out': dense(k[5], (ne, EVEN_MIX_WIDTH, D_MODEL), EVEN_MIX_WIDTH),
        'gla_w_gate': dense(k[6], (ne, GLA_GATE_RANK, GLA_HEADS * GLA_DK), GLA_GATE_RANK),
        'gla_b_gate': 0.1 * normal(k[7], (ne, GLA_HEADS * GLA_DK)),
        'gla_norm': 1.0 + 0.05 * normal(k[8], (ne, GLA_HEADS, GLA_DV)),
        'nsa_gate_b': 0.1 * normal(k[9], (ne, NSA_HEADS * NSA_BRANCHES)),
        'nsa_cmp_pos': 0.1 * normal(k[10], (ne, 2, NSA_CMP_BLOCK, NSA_DH)),
        'nsa_cmp_w1': dense(k[11], (ne, 2, NSA_CMP_BLOCK * NSA_DH, NSA_CMP_HIDDEN), NSA_CMP_BLOCK * NSA_DH),
        'nsa_cmp_w2': dense(k[12], (ne, 2, NSA_CMP_HIDDEN, NSA_DH), NSA_CMP_HIDDEN),
        'o_w_in': dense(k[13], (no, D_MODEL, 2 * SGU_WIDTH), D_MODEL),
        'o_ln_g': 1.0 + 0.05 * normal(k[14], (no, SGU_WIDTH)),
        'o_ln_b': 0.02 * normal(k[15], (no, SGU_WIDTH)),
        'o_w_s': dense(k[16], (no, SGU_GROUPS, SGU_CHUNK, SGU_CHUNK), SGU_CHUNK),
        'o_b_s': 1.0 + 0.1 * normal(k[17], (no, SGU_GROUPS, SGU_CHUNK)),
        'o_w_out': dense(k[18], (no, SGU_WIDTH, D_MODEL), SGU_WIDTH),
    }


def reference(x, norm_g, ffn_w1, ffn_w2, e_w_in, e_w_out, gla_w_gate, gla_b_gate, gla_norm, nsa_gate_b, nsa_cmp_pos, nsa_cmp_w1, nsa_cmp_w2, o_w_in, o_ln_g, o_ln_b, o_w_s, o_b_s, o_w_out):
    h = x
    for layer in range(DEPTH):
        i = layer // 2
        xn = _rms_norm(h, norm_g[layer, 0])
        if layer % 2 == 0:
            m = _even_mixer(xn, e_w_in[i], e_w_out[i], gla_w_gate[i], gla_b_gate[i], gla_norm[i],
                            nsa_gate_b[i], nsa_cmp_pos[i], nsa_cmp_w1[i], nsa_cmp_w2[i])
        else:
            m = _sgu_mixer(xn, o_w_in[i], o_ln_g[i], o_ln_b[i], o_w_s[i], o_b_s[i], o_w_out[i])
        h = h + _rms_norm(m, norm_g[layer, 1])
        f = _squared_relu_mlp(_rms_norm(h, norm_g[layer, 2]), ffn_w1[layer], ffn_w2[layer])
        h = h + _rms_norm(f, norm_g[layer, 3])
    return h
```

```python
import functools

import jax
import jax.numpy as jnp
from jax import lax
from jax.experimental import pallas as pl
from jax.experimental.pallas import tpu as pltpu

F32 = jnp.float32
BF16 = jnp.bfloat16

D_MODEL = 1024
GLA_HEADS = 4
GLA_DK = 64
GLA_DV = 128
GLA_GATE_RANK = 16
GLA_GATE_TEMP = 16.0
GLA_CHUNK = 64
NSA_HEADS = 8
NSA_KV_GROUPS = 2
NSA_HPG = NSA_HEADS // NSA_KV_GROUPS
NSA_DH = 64
NSA_CMP_BLOCK = 32
NSA_CMP_STRIDE = 16
NSA_CMP_HIDDEN = 128
NSA_SEL_BLOCK = 64
NSA_TOPN = 16
NSA_WINDOW = 512
NSA_QBLOCK = 128
NSA_BRANCHES = 3
SGU_WIDTH = 2 * D_MODEL
SGU_GROUPS = 8
SGU_CHUNK = 128
FFN_HIDDEN = 4 * D_MODEL
NORM_EPS = 1e-6
NEG_BIG = -1e30
POS_BIG = 1e30

GLA_QK = GLA_HEADS * GLA_DK
GLA_V = GLA_HEADS * GLA_DV
NSA_Q = NSA_HEADS * NSA_DH
NSA_KV = NSA_KV_GROUPS * NSA_DH
NSA_GATE_ROWS = 16
LANE = 128

VMEM_LIMIT_BYTES = 56 * 1024 * 1024

ROW_TILE = 512
SGU_ROW_TILE = 256
GLA_ROW_TILE = 256
SEL_KV_TILE = 512


def _cparams(*sem):
    return pltpu.CompilerParams(dimension_semantics=sem, vmem_limit_bytes=VMEM_LIMIT_BYTES)


def _rms(x, g):
    return x * lax.rsqrt(jnp.mean(x * x, axis=-1, keepdims=True) + NORM_EPS) * g


def _dot(a, b):
    return jnp.dot(a, b, preferred_element_type=F32)


def _dot_nt(a, b):
    return lax.dot_general(a, b, (((1,), (1,)), ((), ())), preferred_element_type=F32)


def _split3(x):
    x1 = x.astype(BF16)
    r1 = x - x1.astype(F32)
    x2 = r1.astype(BF16)
    r2 = r1 - x2.astype(F32)
    return x1, x2, r2.astype(BF16)


def _dot_exact_lhs(a_bf16, b_f32):
    b1, b2, b3 = _split3(b_f32)
    return _dot(a_bf16, b1) + _dot(a_bf16, b2) + _dot(a_bf16, b3)


def _const_spec(shape):
    return pl.BlockSpec(shape, lambda *_: (0,) * len(shape))


def _even_proj_kernel(h_ref, g_ref, wrow_ref, wt_ref, wlr_ref, wgate_ref, bgate_ref,
                      gq_ref, gk_ref, gv_ref, gvt_ref, gr_ref, glog_ref,
                      kc_ref, vc_ref, ks_ref, kw_ref, qt_ref, vst_ref, vwt_ref, ngt_ref):
    xn = _rms(h_ref[...], g_ref[...]).astype(BF16)
    o = 0
    gq_ref[...] = _dot(xn, wrow_ref[:, o:o + GLA_QK]) * (GLA_DK ** -0.5)
    o += GLA_QK
    gk_ref[...] = _dot(xn, wrow_ref[:, o:o + GLA_QK])
    o += GLA_QK
    gv_ref[...] = _dot(xn, wrow_ref[:, o:o + GLA_V]).astype(BF16)
    o += GLA_V
    gr_ref[...] = _dot(xn, wrow_ref[:, o:o + GLA_V])
    o += GLA_V
    for ref in (kc_ref, vc_ref, ks_ref, kw_ref):
        y = _dot(xn, wrow_ref[:, o:o + NSA_KV]).astype(BF16)
        for g in range(NSA_KV_GROUPS):
            ref[g] = y[:, g * NSA_DH:(g + 1) * NSA_DH]
        o += NSA_KV
    o = 0
    gvt_ref[...] = _dot_nt(wt_ref[o:o + GLA_V, :], xn).astype(BF16)
    o += GLA_V
    qt_ref[...] = (_dot_nt(wt_ref[o:o + NSA_Q, :], xn) * (NSA_DH ** -0.5)).astype(BF16)
    o += NSA_Q
    vst_ref[...] = _dot_nt(wt_ref[o:o + NSA_KV, :], xn).astype(BF16)
    o += NSA_KV
    vwt_ref[...] = _dot_nt(wt_ref[o:o + NSA_KV, :], xn).astype(BF16)
    o += NSA_KV
    ngt_ref[...] = _dot_nt(wt_ref[o:o + 2 * NSA_GATE_ROWS, :], xn)
    lr = _dot(xn, wlr_ref[...])
    l1, l2, l3 = _split3(lr)
    w1, w2, w3 = _split3(wgate_ref[...])
    z = (_dot(l1, w1) + _dot(l1, w2) + _dot(l2, w1) + _dot(l2, w2) + _dot(l1, w3) + _dot(l3, w1)
         + bgate_ref[...])
    glog_ref[...] = jax.nn.log_sigmoid(z) * (1.0 / GLA_GATE_TEMP)


def _even_proj(h, g, w_in, w_gate, b_gate):
    bsz, seq, d = h.shape
    tm = min(ROW_TILE, seq)
    sizes = (GLA_QK, GLA_QK, GLA_V, GLA_GATE_RANK, GLA_V, NSA_Q) + (NSA_KV,) * 6 + (NSA_HEADS * NSA_BRANCHES,)
    offs = [0]
    for s in sizes:
        offs.append(offs[-1] + s)
    col = lambda i: w_in[:, offs[i]:offs[i + 1]]
    gq, gk, gv, glr, gr, nq, kc, vc, ks, vs, kw, vw, ng = [col(i) for i in range(13)]
    wrow = jnp.concatenate([gq, gk, gv, gr, kc, vc, ks, kw], axis=1).astype(BF16)
    ng_g = ng.reshape(d, NSA_KV_GROUPS, NSA_HPG * NSA_BRANCHES)
    ng_g = jnp.pad(ng_g, ((0, 0), (0, 0), (0, NSA_GATE_ROWS - NSA_HPG * NSA_BRANCHES)))
    wt = jnp.concatenate([gv, nq, vs, vw, ng_g.reshape(d, NSA_KV_GROUPS * NSA_GATE_ROWS)], axis=1).T.astype(BF16)
    wlr = jnp.pad(glr, ((0, 0), (0, LANE - GLA_GATE_RANK))).astype(BF16)
    wgate = jnp.pad(w_gate, ((0, LANE - GLA_GATE_RANK), (0, 0)))
    n_t = wt.shape[0]

    row = lambda w: pl.BlockSpec((None, tm, w), lambda b, i: (b, i, 0))
    grp = pl.BlockSpec((None, NSA_KV_GROUPS, tm, NSA_DH), lambda b, i: (b, 0, i, 0))
    tr = lambda w: pl.BlockSpec((None, w, tm), lambda b, i: (b, 0, i))
    sds = jax.ShapeDtypeStruct
    kv_shape = sds((bsz, NSA_KV_GROUPS, seq, NSA_DH), BF16)
    out_shape = (
        sds((bsz, seq, GLA_QK), F32), sds((bsz, seq, GLA_QK), F32), sds((bsz, seq, GLA_V), BF16),
        sds((bsz, GLA_V, seq), BF16), sds((bsz, seq, GLA_V), F32), sds((bsz, seq, GLA_QK), F32),
        kv_shape, kv_shape, kv_shape, kv_shape,
        sds((bsz, NSA_Q, seq), BF16), sds((bsz, NSA_KV, seq), BF16), sds((bsz, NSA_KV, seq), BF16),
        sds((bsz, NSA_KV_GROUPS * NSA_GATE_ROWS, seq), F32),
    )
    out_specs = (
        row(GLA_QK), row(GLA_QK), row(GLA_V), tr(GLA_V), row(GLA_V), row(GLA_QK),
        grp, grp, grp, grp,
        tr(NSA_Q), tr(NSA_KV), tr(NSA_KV), tr(NSA_KV_GROUPS * NSA_GATE_ROWS),
    )
    return pl.pallas_call(
        _even_proj_kernel,
        grid=(bsz, seq // tm),
        in_specs=[
            row(d), _const_spec((1, d)), _const_spec(wrow.shape), _const_spec((n_t, d)),
            _const_spec((d, LANE)), _const_spec((LANE, GLA_QK)), _const_spec((1, GLA_QK)),
        ],
        out_specs=out_specs,
        out_shape=out_shape,
        compiler_params=_cparams("parallel", "parallel"),
        name="even_proj",
    )(h, g.reshape(1, d), wrow, wt, wlr, wgate, b_gate.reshape(1, GLA_QK))


def _gla_kernel(q_ref, k_ref, v_ref, vt_ref, r_ref, glog_ref, ng_ref, o_ref, state_ref):
    c = GLA_CHUNK
    ts = q_ref.shape[0]

    @pl.when(pl.program_id(1) == 0)
    def _():
        state_ref[...] = jnp.zeros_like(state_ref)

    ri = lax.broadcasted_iota(jnp.int32, (c, c), 0)
    ci = lax.broadcasted_iota(jnp.int32, (c, c), 1)
    causal = ri >= ci
    tril = jnp.where(causal, 1.0, 0.0).astype(BF16)
    row2 = lax.broadcasted_iota(jnp.int32, (2 * c, 1), 0)

    for pair in range(ts // (2 * c)):
        p0 = pair * 2 * c
        vt_pair = vt_ref[:, p0:p0 + 2 * c]
        for half in range(2):
            r0 = p0 + half * c
            gl = glog_ref[r0:r0 + c, :]
            b = _dot_exact_lhs(tril, gl)
            b_last = b[c - 1:c, :]
            q_t = (q_ref[r0:r0 + c, :] * jnp.exp(b)).astype(BF16)
            k = k_ref[r0:r0 + c, :]
            k_t = (k * jnp.exp(-b)).astype(BF16)
            k_end = (k * jnp.exp(b_last - b)).astype(BF16)
            decay = jnp.exp(b_last)
            in_half = (row2 >= half * c) & (row2 < (half + 1) * c)
            for h in range(GLA_HEADS):
                ks = slice(h * GLA_DK, (h + 1) * GLA_DK)
                vs = slice(h * GLA_DV, (h + 1) * GLA_DV)
                a = jnp.where(causal, _dot_nt(q_t[:, ks], k_t[:, ks]), 0.0).astype(BF16)
                st = state_ref[h]
                o = _dot(a, v_ref[r0:r0 + c, vs]) + _dot_nt(q_t[:, ks], st.astype(BF16))
                ke_pair = jnp.where(in_half, jnp.concatenate([k_end[:, ks], k_end[:, ks]], axis=0), 0.0)
                state_ref[h] = st * decay[:, ks] + _dot(vt_pair[vs, :], ke_pair.astype(BF16))
                o = o * lax.rsqrt(jnp.mean(o * o, axis=-1, keepdims=True) + NORM_EPS) * ng_ref[:, vs]
                o_ref[r0:r0 + c, vs] = o * jax.nn.silu(r_ref[r0:r0 + c, vs])


def _gla(gq, gk, gv, gvt, gr, glog, gla_norm):
    bsz, seq, _ = gq.shape
    ts = min(GLA_ROW_TILE, seq)
    row = lambda w: pl.BlockSpec((None, ts, w), lambda b, i: (b, i, 0))
    return pl.pallas_call(
        _gla_kernel,
        grid=(bsz, seq // ts),
        in_specs=[row(GLA_QK), row(GLA_QK), row(GLA_V),
                  pl.BlockSpec((None, GLA_V, ts), lambda b, i: (b, 0, i)),
                  row(GLA_V), row(GLA_QK), _const_spec((1, GLA_V))],
        out_specs=row(GLA_V),
        out_shape=jax.ShapeDtypeStruct((bsz, seq, GLA_V), F32),
        scratch_shapes=[pltpu.VMEM((GLA_HEADS, GLA_DV, GLA_DK), F32)],
        compiler_params=_cparams("parallel", "arbitrary"),
        name="gla",
    )(gq, gk, gv, gvt, gr, glog, gla_norm.reshape(1, GLA_V))


def _nsa_compress_kernel(kc_ref, vc_ref, pos_ref, w1_ref, w2_ref, w2t_ref, kcmp_ref, vcmpt_ref):
    half = (NSA_CMP_BLOCK // 2) * NSA_DH

    def hidden(x_ref, i):
        x = x_ref[...]
        first = _dot(x, w1_ref[i, :half, :])
        second = _dot(x, w1_ref[i, half:, :])
        nr = first.shape[0]
        nxt = jnp.concatenate([second[1:], second[:1]], axis=0)
        posc = _dot(pos_ref[i], w1_ref[i])[0:1]
        return jax.nn.gelu(first + nxt + posc).astype(BF16)

    kcmp_ref[...] = _dot(hidden(kc_ref, 0), w2_ref[0]).astype(BF16)
    vcmpt_ref[...] = _dot_nt(w2t_ref[...], hidden(vc_ref, 1)).astype(BF16)


def _nsa_compress(kc, vc, cmp_pos, cmp_w1, cmp_w2):
    bsz, g_n, seq, dh = kc.shape
    grp_tok = NSA_CMP_BLOCK // 2
    nr = seq // grp_tok
    feat = grp_tok * dh
    kc_r = kc.reshape(bsz, g_n, nr, feat)
    vc_r = vc.reshape(bsz, g_n, nr, feat)
    pos = jnp.broadcast_to(cmp_pos.reshape(2, 1, NSA_CMP_BLOCK * dh), (2, 8, NSA_CMP_BLOCK * dh)).astype(BF16)
    blk = pl.BlockSpec((None, None, nr, feat), lambda b, g: (b, g, 0, 0))
    return pl.pallas_call(
        _nsa_compress_kernel,
        grid=(bsz, g_n),
        in_specs=[blk, blk, _const_spec(pos.shape), _const_spec(cmp_w1.shape), _const_spec(cmp_w2.shape),
                  _const_spec((dh, NSA_CMP_HIDDEN))],
        out_specs=(pl.BlockSpec((None, None, nr, dh), lambda b, g: (b, g, 0, 0)),
                   pl.BlockSpec((None, None, dh, nr), lambda b, g: (b, g, 0, 0))),
        out_shape=(jax.ShapeDtypeStruct((bsz, g_n, nr, dh), BF16),
                   jax.ShapeDtypeStruct((bsz, g_n, dh, nr), BF16)),
        compiler_params=_cparams("parallel", "parallel"),
        name="nsa_compress",
    )(kc_r, vc_r, pos, cmp_w1.astype(BF16), cmp_w2.astype(BF16), cmp_w2[1].T.astype(BF16))


def _softmax_cols(s, ok):
    s = jnp.where(ok, s, NEG_BIG)
    p = jnp.where(ok, jnp.exp(s - jnp.max(s, axis=0, keepdims=True)), 0.0)
    return p / jnp.maximum(jnp.sum(p, axis=0, keepdims=True), 1e-30)


def _nsa_attn_kernel(qt_ref, gt_ref, gb_ref, kcmp_ref, vcmpt_ref, ovl_ref, ks_ref, vst_ref, kw_ref, vwt_ref,
                     o_ref, sel_ref):
    qb, dh, hpg, sb = NSA_QBLOCK, NSA_DH, NSA_HPG, NSA_SEL_BLOCK
    n = pl.program_id(2)
    t0 = n * qb
    nr = kcmp_ref.shape[0]
    n_sel = sel_ref.shape[0]
    seq = ks_ref.shape[0]
    heads = range(hpg)
    hs = lambda h: slice(h * qb, (h + 1) * qb)

    q = jnp.concatenate([qt_ref[h * dh:(h + 1) * dh, :] for h in heads], axis=1)
    t = t0 + lax.broadcasted_iota(jnp.int32, (1, qb), 1)

    s_c = _dot(kcmp_ref[...], q)
    cmp_end = lax.broadcasted_iota(jnp.int32, (nr, 1), 0) * NSA_CMP_STRIDE + (NSA_CMP_BLOCK - 1)
    ok_c = cmp_end <= t
    p_c = [_softmax_cols(s_c[:, hs(h)], ok_c) for h in heads]
    o_c = _dot(vcmpt_ref[...], jnp.concatenate(p_c, axis=1).astype(BF16))

    imp = _dot_exact_lhs(ovl_ref[...], sum(p_c[1:], p_c[0]))
    j = lax.broadcasted_iota(jnp.int32, (n_sel, 1), 0)
    jf = j.astype(F32)
    cur = t >> 6
    forced = (j == 0) | (j == cur) | (j == cur - 1)
    score = jnp.where(forced, POS_BIG, jnp.where(j <= cur, imp, NEG_BIG))
    sel = jnp.zeros((n_sel, qb), F32)
    for _ in range(min(NSA_TOPN, n_sel)):
        top = jnp.max(score, axis=0, keepdims=True)
        first = jnp.min(jnp.where(score == top, jf, float(n_sel)), axis=0, keepdims=True)
        pick = jf == first
        sel = jnp.where(pick & (top > 0.5 * NEG_BIG), 1.0, sel)
        score = jnp.where(pick, -jnp.inf, score)
    sel_ref[...] = sel

    tk = SEL_KV_TILE
    bpt = tk // sb
    kpos_tile = lax.broadcasted_iota(jnp.int32, (tk, 1), 0)

    def sel_step(kt, carry):
        m, l, acc = carry
        k0 = pl.multiple_of(kt * tk, tk)
        s = _dot(ks_ref[pl.ds(k0, tk), :], q)
        rows = sel_ref[pl.ds(pl.multiple_of(kt * bpt, bpt), bpt), :]
        blk = jnp.concatenate([jnp.broadcast_to(rows[i:i + 1, :], (sb, qb)) for i in range(bpt)], axis=0)
        ok = (blk > 0.5) & (kpos_tile + k0 <= t)
        m_new, p = [], []
        for h in heads:
            s_h = jnp.where(ok, s[:, hs(h)], NEG_BIG)
            m_h = jnp.maximum(m[:, hs(h)], jnp.max(s_h, axis=0, keepdims=True))
            p.append(jnp.where(ok, jnp.exp(s_h - m_h), 0.0))
            m_new.append(m_h)
        m_new = jnp.concatenate(m_new, axis=1)
        p = jnp.concatenate(p, axis=1)
        alpha = jnp.exp(m - m_new)
        l = alpha * l + jnp.sum(p, axis=0, keepdims=True)
        acc = alpha * acc + _dot(vst_ref[:, pl.ds(k0, tk)], p.astype(BF16))
        return m_new, l, acc

    n_tiles = (n + tk // qb) // (tk // qb)
    init = (jnp.full((1, hpg * qb), NEG_BIG, F32), jnp.zeros((1, hpg * qb), F32), jnp.zeros((dh, hpg * qb), F32))
    _, l_s, acc_s = lax.fori_loop(0, n_tiles, sel_step, init)
    o_s = acc_s / jnp.maximum(l_s, 1e-30)

    wk = min(NSA_WINDOW + qb, seq)
    start = pl.multiple_of(jnp.maximum(t0 + qb - wk, 0), qb)
    s_w = _dot(kw_ref[pl.ds(start, wk), :], q)
    kpos = start + lax.broadcasted_iota(jnp.int32, (wk, 1), 0)
    ok_w = (kpos <= t) & (kpos > t - NSA_WINDOW)
    p_w = jnp.concatenate([_softmax_cols(s_w[:, hs(h)], ok_w) for h in heads], axis=1)
    o_w = _dot(vwt_ref[:, pl.ds(start, wk)], p_w.astype(BF16))

    gates = jax.nn.sigmoid(gt_ref[...] + gb_ref[...])
    for h in heads:
        g0 = gates[h * NSA_BRANCHES + 0:h * NSA_BRANCHES + 1, :]
        g1 = gates[h * NSA_BRANCHES + 1:h * NSA_BRANCHES + 2, :]
        g2 = gates[h * NSA_BRANCHES + 2:h * NSA_BRANCHES + 3, :]
        o_ref[h * dh:(h + 1) * dh, :] = g0 * o_c[:, hs(h)] + g1 * o_s[:, hs(h)] + g2 * o_w[:, hs(h)]


def _nsa_attn(qt, ngt, gate_b, kcmp, vcmpt, ks, vst, kw, vwt):
    bsz, g_n, seq, dh = ks.shape
    nr = kcmp.shape[2]
    n_sel = seq // NSA_SEL_BLOCK
    n_qb = seq // NSA_QBLOCK
    n_cmp = (seq - NSA_CMP_BLOCK) // NSA_CMP_STRIDE + 1
    cmp_start = jnp.arange(nr) * NSA_CMP_STRIDE
    sel_start = jnp.arange(n_sel) * NSA_SEL_BLOCK
    ovl = ((cmp_start[None, :] < sel_start[:, None] + NSA_SEL_BLOCK)
           & (cmp_start[None, :] + NSA_CMP_BLOCK > sel_start[:, None])
           & (jnp.arange(nr)[None, :] < n_cmp)).astype(BF16)
    gb = jnp.pad(gate_b.reshape(g_n, NSA_HPG * NSA_BRANCHES), ((0, 0), (0, NSA_GATE_ROWS - NSA_HPG * NSA_BRANCHES)))
    gb = gb.reshape(g_n, NSA_GATE_ROWS, 1)
    qrows = NSA_HPG * dh
    full = lambda a, c: pl.BlockSpec((None, None, a, c), lambda b, g, n: (b, g, 0, 0))
    return pl.pallas_call(
        _nsa_attn_kernel,
        grid=(bsz, g_n, n_qb),
        in_specs=[
            pl.BlockSpec((None, qrows, NSA_QBLOCK), lambda b, g, n: (b, g, n)),
            pl.BlockSpec((None, NSA_GATE_ROWS, NSA_QBLOCK), lambda b, g, n: (b, g, n)),
            pl.BlockSpec((None, NSA_GATE_ROWS, 1), lambda b, g, n: (g, 0, 0)),
            full(nr, dh), full(dh, nr), _const_spec((n_sel, nr)),
            full(seq, dh), pl.BlockSpec((None, dh, seq), lambda b, g, n: (b, g, 0)),
            full(seq, dh), pl.BlockSpec((None, dh, seq), lambda b, g, n: (b, g, 0)),
        ],
        out_specs=pl.BlockSpec((None, qrows, NSA_QBLOCK), lambda b, g, n: (b, g, n)),
        out_shape=jax.ShapeDtypeStruct((bsz, g_n * qrows, seq), F32),
        scratch_shapes=[pltpu.VMEM((n_sel, NSA_QBLOCK), F32)],
        compiler_params=_cparams("parallel", "parallel", "arbitrary"),
        name="nsa_attn",
    )(qt, ngt, gb, kcmp, vcmpt, ovl, ks, vst, kw, vwt)


def _even_out_kernel(h_ref, oa_ref, ob_ref, w_ref, g_ref, o_ref):
    m = _dot(oa_ref[...].astype(BF16), w_ref[:GLA_V, :]) + _dot(ob_ref[...].astype(BF16), w_ref[GLA_V:, :])
    o_ref[...] = h_ref[...] + _rms(m, g_ref[...])


def _even_out(h, o_a, o_b, w_out, g):
    bsz, seq, d = h.shape
    tm = min(ROW_TILE, seq)
    row = lambda w: pl.BlockSpec((None, tm, w), lambda b, i: (b, i, 0))
    return pl.pallas_call(
        _even_out_kernel,
        grid=(bsz, seq // tm),
        in_specs=[row(d), row(GLA_V), row(NSA_Q), _const_spec(w_out.shape), _const_spec((1, d))],
        out_specs=row(d),
        out_shape=jax.ShapeDtypeStruct(h.shape, F32),
        compiler_params=_cparams("parallel", "parallel"),
        name="even_out",
    )(h, o_a, o_b, w_out.astype(BF16), g.reshape(1, d))


def _ffn_kernel(h_ref, g_in_ref, w1_ref, w2_ref, g_out_ref, o_ref):
    h = h_ref[...]
    xn = _rms(h, g_in_ref[...]).astype(BF16)
    hid = w1_ref.shape[1]
    step = 1024
    acc = jnp.zeros(h.shape, F32)
    for j in range(0, hid, step):
        a = jnp.maximum(_dot(xn, w1_ref[:, j:j + step]), 0.0)
        acc = acc + _dot((a * a).astype(BF16), w2_ref[j:j + step, :])
    o_ref[...] = h + _rms(acc, g_out_ref[...])


def _ffn(h, g_in, w1, w2, g_out):
    bsz, seq, d = h.shape
    tm = min(ROW_TILE, seq)
    row = pl.BlockSpec((None, tm, d), lambda b, i: (b, i, 0))
    return pl.pallas_call(
        _ffn_kernel,
        grid=(bsz, seq // tm),
        in_specs=[row, _const_spec((1, d)), _const_spec(w1.shape), _const_spec(w2.shape), _const_spec((1, d))],
        out_specs=row,
        out_shape=jax.ShapeDtypeStruct(h.shape, F32),
        compiler_params=_cparams("parallel", "parallel"),
        name="ffn",
    )(h, g_in.reshape(1, d), w1.astype(BF16), w2.astype(BF16), g_out.reshape(1, d))


def _sgu_kernel(h_ref, g_in_ref, w_in_ref, ln_g_ref, ln_b_ref, ws_ref, bs_ref, w_out_ref, g_out_ref, o_ref):
    e, c, gw = SGU_WIDTH, SGU_CHUNK, SGU_WIDTH // SGU_GROUPS
    h = h_ref[...]
    tm = h.shape[0]
    xn = _rms(h, g_in_ref[...]).astype(BF16)
    u = jax.nn.gelu(_dot(xn, w_in_ref[:, :e]))
    v = jax.nn.gelu(_dot(xn, w_in_ref[:, e:]))
    mu = jnp.mean(v, axis=-1, keepdims=True)
    var = jnp.mean((v - mu) ** 2, axis=-1, keepdims=True)
    vn = ((v - mu) * lax.rsqrt(var + NORM_EPS) * ln_g_ref[...] + ln_b_ref[...]).astype(BF16)
    causal = lax.broadcasted_iota(jnp.int32, (c, c), 0) >= lax.broadcasted_iota(jnp.int32, (c, c), 1)
    w_c = [jnp.where(causal, ws_ref[g], 0.0).astype(BF16) for g in range(SGU_GROUPS)]
    rows = []
    for r0 in range(0, tm, c):
        mixed = [_dot(w_c[g], vn[r0:r0 + c, g * gw:(g + 1) * gw]) for g in range(SGU_GROUPS)]
        rows.append(jnp.concatenate(mixed, axis=1) + bs_ref[...])
    y = (u * jnp.concatenate(rows, axis=0)).astype(BF16)
    o_ref[...] = h + _rms(_dot(y, w_out_ref[...]), g_out_ref[...])


def _sgu(h, g_in, w_in, ln_g, ln_b, w_s, b_s, w_out, g_out):
    bsz, seq, d = h.shape
    tm = min(SGU_ROW_TILE, seq)
    e = SGU_WIDTH
    bias = jnp.repeat(b_s.T, e // SGU_GROUPS, axis=1)
    row = pl.BlockSpec((None, tm, d), lambda b, i: (b, i, 0))
    return pl.pallas_call(
        _sgu_kernel,
        grid=(bsz, seq // tm),
        in_specs=[row, _const_spec((1, d)), _const_spec(w_in.shape), _const_spec((1, e)), _const_spec((1, e)),
                  _const_spec(w_s.shape), _const_spec(bias.shape), _const_spec(w_out.shape), _const_spec((1, d))],
        out_specs=row,
        out_shape=jax.ShapeDtypeStruct(h.shape, F32),
        compiler_params=_cparams("parallel", "parallel"),
        name="sgu",
    )(h, g_in.reshape(1, d), w_in.astype(BF16), ln_g.reshape(1, e), ln_b.reshape(1, e), w_s, bias,
      w_out.astype(BF16), g_out.reshape(1, d))


def _even_mixer(h, g_in, g_out, w_in, w_out, w_gate, b_gate, gla_norm, gate_b, cmp_pos, cmp_w1, cmp_w2):
    bsz, seq, _ = h.shape
    (gq, gk, gv, gvt, gr, glog, kc, vc, ks, kw, qt, vst, vwt, ngt) = _even_proj(h, g_in, w_in, w_gate, b_gate)
    o_a = _gla(gq, gk, gv, gvt, gr, glog, gla_norm)
    kcmp, vcmpt = _nsa_compress(kc, vc, cmp_pos, cmp_w1, cmp_w2)
    o_bt = _nsa_attn(qt, ngt, gate_b, kcmp, vcmpt, ks, vst, kw, vwt)
    o_b = jnp.swapaxes(o_bt, 1, 2)
    return _even_out(h, o_a, o_b, w_out, g_out)


def kernel(x, norm_g, ffn_w1, ffn_w2, e_w_in, e_w_out, gla_w_gate, gla_b_gate, gla_norm, nsa_gate_b, nsa_cmp_pos, nsa_cmp_w1, nsa_cmp_w2, o_w_in, o_ln_g, o_ln_b, o_w_s, o_b_s, o_w_out):
    h = x
    depth = norm_g.shape[0]
    for layer in range(depth):
        i = layer // 2
        if layer % 2 == 0:
            h = _even_mixer(h, norm_g[layer, 0], norm_g[layer, 1], e_w_in[i], e_w_out[i], gla_w_gate[i],
                            gla_b_gate[i], gla_norm[i], nsa_gate_b[i], nsa_cmp_pos[i], nsa_cmp_w1[i], nsa_cmp_w2[i])
        else:
            h = _sgu(h, norm_g[layer, 0], o_w_in[i], o_ln_g[i], o_ln_b[i], o_w_s[i], o_b_s[i], o_w_out[i],
                     norm_g[layer, 1])
        h = _ffn(h, norm_g[layer, 2], ffn_w1[layer], ffn_w2[layer], norm_g[layer, 3])
    return h
```

```python
import functools

import jax
import jax.numpy as jnp
from jax import lax
from jax.experimental import pallas as pl
from jax.experimental.pallas import tpu as pltpu

F32 = jnp.float32
BF16 = jnp.bfloat16

D_MODEL = 1024
GLA_HEADS = 4
GLA_DK = 64
GLA_DV = 128
GLA_GATE_RANK = 16
GLA_GATE_TEMP = 16.0
GLA_CHUNK = 64
NSA_HEADS = 8
NSA_KV_GROUPS = 2
NSA_HPG = NSA_HEADS // NSA_KV_GROUPS
NSA_DH = 64
NSA_CMP_BLOCK = 32
NSA_CMP_STRIDE = 16
NSA_CMP_HIDDEN = 128
NSA_SEL_BLOCK = 64
NSA_TOPN = 16
NSA_WINDOW = 512
NSA_QBLOCK = 128
NSA_BRANCHES = 3
SGU_WIDTH = 2 * D_MODEL
SGU_GROUPS = 8
SGU_CHUNK = 128
FFN_HIDDEN = 4 * D_MODEL
NORM_EPS = 1e-6
NEG_BIG = -1e30
POS_BIG = 1e30

GLA_QK = GLA_HEADS * GLA_DK
GLA_V = GLA_HEADS * GLA_DV
NSA_Q = NSA_HEADS * NSA_DH
NSA_KV = NSA_KV_GROUPS * NSA_DH
NSA_GATE_ROWS = 16
NSA_VROWS = NSA_DH + 16
LOG2E = 1.4426950408889634
LANE = 128

VMEM_LIMIT_BYTES = 56 * 1024 * 1024

ROW_TILE = 512
SGU_ROW_TILE = 256
GLA_ROW_TILE = 256
SEL_KV_TILE = 512
SEL_QBLOCKS = 4


def _cparams(*sem):
    return pltpu.CompilerParams(dimension_semantics=sem, vmem_limit_bytes=VMEM_LIMIT_BYTES)


def _rms(x, g):
    return x * lax.rsqrt(jnp.mean(x * x, axis=-1, keepdims=True) + NORM_EPS) * g


def _dot(a, b):
    return jnp.dot(a, b, preferred_element_type=F32)


def _dot_nt(a, b):
    return lax.dot_general(a, b, (((1,), (1,)), ((), ())), preferred_element_type=F32)


def _split3(x):
    x1 = x.astype(BF16)
    r1 = x - x1.astype(F32)
    x2 = r1.astype(BF16)
    r2 = r1 - x2.astype(F32)
    return x1, x2, r2.astype(BF16)


def _dot_exact_lhs(a_bf16, b_f32):
    b1, b2, b3 = _split3(b_f32)
    return _dot(a_bf16, b1) + _dot(a_bf16, b2) + _dot(a_bf16, b3)


def _const_spec(shape):
    return pl.BlockSpec(shape, lambda *_: (0,) * len(shape))


def _even_proj_kernel(h_ref, g_ref, wrow_ref, wt_ref, wlr_ref, wgate_ref, bgate_ref,
                      gq_ref, gk_ref, gv_ref, gvt_ref, gr_ref, glog_ref,
                      kc_ref, vc_ref, ks_ref, kw_ref, qt_ref, vst_ref, vwt_ref, ngt_ref):
    xn = _rms(h_ref[...], g_ref[...]).astype(BF16)
    o = 0
    gq_ref[...] = _dot(xn, wrow_ref[:, o:o + GLA_QK]) * (GLA_DK ** -0.5)
    o += GLA_QK
    gk_ref[...] = _dot(xn, wrow_ref[:, o:o + GLA_QK])
    o += GLA_QK
    gv_ref[...] = _dot(xn, wrow_ref[:, o:o + GLA_V]).astype(BF16)
    o += GLA_V
    gr_ref[...] = _dot(xn, wrow_ref[:, o:o + GLA_V])
    o += GLA_V
    tm = xn.shape[0]
    for ref in (kc_ref, vc_ref, kw_ref):
        y = _dot(xn, wrow_ref[:, o:o + NSA_KV]).astype(BF16)
        for g in range(NSA_KV_GROUPS):
            ref[g] = y[:, g * NSA_DH:(g + 1) * NSA_DH]
        o += NSA_KV
    y = _dot(xn, wrow_ref[:, o:o + NSA_KV])
    lane = lax.broadcasted_iota(jnp.int32, (tm, NSA_DH), 1)
    blk = lax.broadcasted_iota(jnp.int32, (tm, NSA_DH), 0) // NSA_SEL_BLOCK
    onehot = jnp.where(lane == blk % (SEL_KV_TILE // NSA_SEL_BLOCK), 1.0, 0.0)
    for g in range(NSA_KV_GROUPS):
        ks_ref[g] = jnp.concatenate([y[:, g * NSA_DH:(g + 1) * NSA_DH], onehot], axis=1).astype(BF16)
    o = 0
    gvt_ref[...] = _dot_nt(wt_ref[o:o + GLA_V, :], xn).astype(BF16)
    o += GLA_V
    qt_ref[...] = (_dot_nt(wt_ref[o:o + NSA_Q, :], xn) * (NSA_DH ** -0.5 * LOG2E)).astype(BF16)
    o += NSA_Q
    ones_row = jnp.where(lax.broadcasted_iota(jnp.int32, (NSA_VROWS - NSA_DH, tm), 0) == 0, 1.0, 0.0)
    for ref in (vst_ref, vwt_ref):
        y = _dot_nt(wt_ref[o:o + NSA_KV, :], xn)
        for g in range(NSA_KV_GROUPS):
            ref[g] = jnp.concatenate([y[g * NSA_DH:(g + 1) * NSA_DH], ones_row], axis=0).astype(BF16)
        o += NSA_KV
    ngt_ref[...] = _dot_nt(wt_ref[o:o + 2 * NSA_GATE_ROWS, :], xn)
    lr = _dot(xn, wlr_ref[...])
    l1, l2, l3 = _split3(lr)
    w1, w2, w3 = _split3(wgate_ref[...])
    z = (_dot(l1, w1) + _dot(l1, w2) + _dot(l2, w1) + _dot(l2, w2) + _dot(l1, w3) + _dot(l3, w1)
         + bgate_ref[...])
    glog_ref[...] = jax.nn.log_sigmoid(z) * (1.0 / GLA_GATE_TEMP)


def _even_proj(h, g, w_in, w_gate, b_gate):
    bsz, seq, d = h.shape
    tm = min(ROW_TILE, seq)
    sizes = (GLA_QK, GLA_QK, GLA_V, GLA_GATE_RANK, GLA_V, NSA_Q) + (NSA_KV,) * 6 + (NSA_HEADS * NSA_BRANCHES,)
    offs = [0]
    for s in sizes:
        offs.append(offs[-1] + s)
    col = lambda i: w_in[:, offs[i]:offs[i + 1]]
    gq, gk, gv, glr, gr, nq, kc, vc, ks, vs, kw, vw, ng = [col(i) for i in range(13)]
    wrow = jnp.concatenate([gq, gk, gv, gr, kc, vc, kw, ks], axis=1).astype(BF16)
    ng_g = ng.reshape(d, NSA_KV_GROUPS, NSA_HPG * NSA_BRANCHES)
    ng_g = jnp.pad(ng_g, ((0, 0), (0, 0), (0, NSA_GATE_ROWS - NSA_HPG * NSA_BRANCHES)))
    wt = jnp.concatenate([gv, nq, vs, vw, ng_g.reshape(d, NSA_KV_GROUPS * NSA_GATE_ROWS)], axis=1).T.astype(BF16)
    wlr = jnp.pad(glr, ((0, 0), (0, LANE - GLA_GATE_RANK))).astype(BF16)
    wgate = jnp.pad(w_gate, ((0, LANE - GLA_GATE_RANK), (0, 0)))
    n_t = wt.shape[0]

    row = lambda w: pl.BlockSpec((None, tm, w), lambda b, i: (b, i, 0))
    grp = pl.BlockSpec((None, NSA_KV_GROUPS, tm, NSA_DH), lambda b, i: (b, 0, i, 0))
    tr = lambda w: pl.BlockSpec((None, w, tm), lambda b, i: (b, 0, i))
    grp_ext = pl.BlockSpec((None, NSA_KV_GROUPS, tm, LANE), lambda b, i: (b, 0, i, 0))
    grp_tr = pl.BlockSpec((None, NSA_KV_GROUPS, NSA_VROWS, tm), lambda b, i: (b, 0, 0, i))
    sds = jax.ShapeDtypeStruct
    kv_shape = sds((bsz, NSA_KV_GROUPS, seq, NSA_DH), BF16)
    vt_shape = sds((bsz, NSA_KV_GROUPS, NSA_VROWS, seq), BF16)
    out_shape = (
        sds((bsz, seq, GLA_QK), F32), sds((bsz, seq, GLA_QK), F32), sds((bsz, seq, GLA_V), BF16),
        sds((bsz, GLA_V, seq), BF16), sds((bsz, seq, GLA_V), F32), sds((bsz, seq, GLA_QK), F32),
        kv_shape, kv_shape, sds((bsz, NSA_KV_GROUPS, seq, LANE), BF16), kv_shape,
        sds((bsz, NSA_Q, seq), BF16), vt_shape, vt_shape,
        sds((bsz, NSA_KV_GROUPS * NSA_GATE_ROWS, seq), F32),
    )
    out_specs = (
        row(GLA_QK), row(GLA_QK), row(GLA_V), tr(GLA_V), row(GLA_V), row(GLA_QK),
        grp, grp, grp_ext, grp,
        tr(NSA_Q), grp_tr, grp_tr, tr(NSA_KV_GROUPS * NSA_GATE_ROWS),
    )
    return pl.pallas_call(
        _even_proj_kernel,
        grid=(bsz, seq // tm),
        in_specs=[
            row(d), _const_spec((1, d)), _const_spec(wrow.shape), _const_spec((n_t, d)),
            _const_spec((d, LANE)), _const_spec((LANE, GLA_QK)), _const_spec((1, GLA_QK)),
        ],
        out_specs=out_specs,
        out_shape=out_shape,
        compiler_params=_cparams("parallel", "parallel"),
        name="even_proj",
    )(h, g.reshape(1, d), wrow, wt, wlr, wgate, b_gate.reshape(1, GLA_QK))


def _gla_kernel(q_ref, k_ref, v_ref, vt_ref, r_ref, glog_ref, ng_ref, o_ref, state_ref):
    c = GLA_CHUNK
    ts = q_ref.shape[0]

    @pl.when(pl.program_id(1) == 0)
    def _():
        state_ref[...] = jnp.zeros_like(state_ref)

    ri = lax.broadcasted_iota(jnp.int32, (c, c), 0)
    ci = lax.broadcasted_iota(jnp.int32, (c, c), 1)
    causal = ri >= ci
    tril = jnp.where(causal, 1.0, 0.0).astype(BF16)
    row2 = lax.broadcasted_iota(jnp.int32, (2 * c, 1), 0)

    for pair in range(ts // (2 * c)):
        p0 = pair * 2 * c
        vt_pair = vt_ref[:, p0:p0 + 2 * c]
        for half in range(2):
            r0 = p0 + half * c
            gl = glog_ref[r0:r0 + c, :]
            b = _dot_exact_lhs(tril, gl)
            b_last = b[c - 1:c, :]
            q_t = (q_ref[r0:r0 + c, :] * jnp.exp(b)).astype(BF16)
            k = k_ref[r0:r0 + c, :]
            k_t = (k * jnp.exp(-b)).astype(BF16)
            k_end = (k * jnp.exp(b_last - b)).astype(BF16)
            decay = jnp.exp(b_last)
            in_half = (row2 >= half * c) & (row2 < (half + 1) * c)
            for h in range(GLA_HEADS):
                ks = slice(h * GLA_DK, (h + 1) * GLA_DK)
                vs = slice(h * GLA_DV, (h + 1) * GLA_DV)
                a = jnp.where(causal, _dot_nt(q_t[:, ks], k_t[:, ks]), 0.0).astype(BF16)
                st = state_ref[h]
                o = _dot(a, v_ref[r0:r0 + c, vs]) + _dot_nt(q_t[:, ks], st.astype(BF16))
                ke_pair = jnp.where(in_half, jnp.concatenate([k_end[:, ks], k_end[:, ks]], axis=0), 0.0)
                state_ref[h] = st * decay[:, ks] + _dot(vt_pair[vs, :], ke_pair.astype(BF16))
                o = o * lax.rsqrt(jnp.mean(o * o, axis=-1, keepdims=True) + NORM_EPS) * ng_ref[:, vs]
                o_ref[r0:r0 + c, vs] = o * jax.nn.silu(r_ref[r0:r0 + c, vs])


def _gla(gq, gk, gv, gvt, gr, glog, gla_norm):
    bsz, seq, _ = gq.shape
    ts = min(GLA_ROW_TILE, seq)
    row = lambda w: pl.BlockSpec((None, ts, w), lambda b, i: (b, i, 0))
    return pl.pallas_call(
        _gla_kernel,
        grid=(bsz, seq // ts),
        in_specs=[row(GLA_QK), row(GLA_QK), row(GLA_V),
                  pl.BlockSpec((None, GLA_V, ts), lambda b, i: (b, 0, i)),
                  row(GLA_V), row(GLA_QK), _const_spec((1, GLA_V))],
        out_specs=row(GLA_V),
        out_shape=jax.ShapeDtypeStruct((bsz, seq, GLA_V), F32),
        scratch_shapes=[pltpu.VMEM((GLA_HEADS, GLA_DV, GLA_DK), F32)],
        compiler_params=_cparams("parallel", "arbitrary"),
        name="gla",
    )(gq, gk, gv, gvt, gr, glog, gla_norm.reshape(1, GLA_V))


def _nsa_compress_kernel(kc_ref, vc_ref, pos_ref, w1_ref, w2_ref, w2t_ref, kcmp_ref, vcmpt_ref):
    half = (NSA_CMP_BLOCK // 2) * NSA_DH

    def hidden(x_ref, i):
        x = x_ref[...]
        first = _dot(x, w1_ref[i, :half, :])
        second = _dot(x, w1_ref[i, half:, :])
        nr = first.shape[0]
        nxt = jnp.concatenate([second[1:], second[:1]], axis=0)
        posc = _dot(pos_ref[i], w1_ref[i])[0:1]
        return jax.nn.gelu(first + nxt + posc).astype(BF16)

    kcmp_ref[...] = _dot(hidden(kc_ref, 0), w2_ref[0]).astype(BF16)
    vcmpt_ref[...] = _dot_nt(w2t_ref[...], hidden(vc_ref, 1)).astype(BF16)


def _nsa_compress(kc, vc, cmp_pos, cmp_w1, cmp_w2):
    bsz, g_n, seq, dh = kc.shape
    grp_tok = NSA_CMP_BLOCK // 2
    nr = seq // grp_tok
    feat = grp_tok * dh
    kc_r = kc.reshape(bsz, g_n, nr, feat)
    vc_r = vc.reshape(bsz, g_n, nr, feat)
    pos = jnp.broadcast_to(cmp_pos.reshape(2, 1, NSA_CMP_BLOCK * dh), (2, 8, NSA_CMP_BLOCK * dh)).astype(BF16)
    blk = pl.BlockSpec((None, None, nr, feat), lambda b, g: (b, g, 0, 0))
    return pl.pallas_call(
        _nsa_compress_kernel,
        grid=(bsz, g_n),
        in_specs=[blk, blk, _const_spec(pos.shape), _const_spec(cmp_w1.shape), _const_spec(cmp_w2.shape),
                  _const_spec((dh, NSA_CMP_HIDDEN))],
        out_specs=(pl.BlockSpec((None, None, nr, dh), lambda b, g: (b, g, 0, 0)),
                   pl.BlockSpec((None, None, dh, nr), lambda b, g: (b, g, 0, 0))),
        out_shape=(jax.ShapeDtypeStruct((bsz, g_n, nr, dh), BF16),
                   jax.ShapeDtypeStruct((bsz, g_n, dh, nr), BF16)),
        compiler_params=_cparams("parallel", "parallel"),
        name="nsa_compress",
    )(kc_r, vc_r, pos, cmp_w1.astype(BF16), cmp_w2.astype(BF16), cmp_w2[1].T.astype(BF16))


def _heads_on_lanes(qt_ref, row0=0):
    return jnp.concatenate([qt_ref[row0 + h * NSA_DH:row0 + (h + 1) * NSA_DH, :] for h in range(NSA_HPG)], axis=1)


def _gate_row(gates, h, branch):
    r = h * NSA_BRANCHES + branch
    return gates[r:r + 1, :]


def _nsa_select_kernel(qt_ref, gt_ref, gb_ref, kcmp_ref, vcmpt_ref, ovl_ref, oc_ref, neg_ref):
    dh, hpg = NSA_DH, NSA_HPG
    w = qt_ref.shape[1]
    nr = kcmp_ref.shape[0]
    n_sel = neg_ref.shape[0]
    hs = lambda h: slice(h * w, (h + 1) * w)
    q = _heads_on_lanes(qt_ref)
    t = pl.program_id(2) * w + lax.broadcasted_iota(jnp.int32, (1, w), 1)

    s_c = _dot(kcmp_ref[...], q)
    cmp_end = lax.broadcasted_iota(jnp.int32, (nr, 1), 0) * NSA_CMP_STRIDE + (NSA_CMP_BLOCK - 1)
    ok_c = cmp_end <= t
    has_key = t >= NSA_CMP_BLOCK - 1
    p_c = []
    for h in range(hpg):
        s_h = jnp.where(ok_c, s_c[:, hs(h)], NEG_BIG)
        e = jnp.exp2(s_h - jnp.max(s_h, axis=0, keepdims=True))
        p_c.append(e * jnp.where(has_key, 1.0 / jnp.sum(e, axis=0, keepdims=True), 0.0))
    o_c = _dot(vcmpt_ref[...], jnp.concatenate(p_c, axis=1).astype(BF16))

    p1, p2, _ = _split3(sum(p_c[1:], p_c[0]))
    imp = _dot(ovl_ref[...], p1) + _dot(ovl_ref[...], p2)
    j = lax.broadcasted_iota(jnp.int32, (n_sel, 1), 0)
    jf = j.astype(F32)
    cur = t // NSA_SEL_BLOCK
    forced = (j == 0) | (j == cur) | (j == cur - 1)
    score = jnp.where(forced, POS_BIG, jnp.where(j <= cur, imp, NEG_BIG))
    for _ in range(min(NSA_TOPN, n_sel)):
        top = jnp.max(score, axis=0, keepdims=True)
        first = jnp.min(jnp.where(score == top, jf, float(n_sel)), axis=0, keepdims=True)
        score = jnp.where(jf == first, -jnp.inf, score)
    neg_ref[...] = jnp.where((score == -jnp.inf) & (j <= cur), 0.0, NEG_BIG)

    gates = jax.nn.sigmoid(gt_ref[...] + gb_ref[...])
    for h in range(hpg):
        oc_ref[h * dh:(h + 1) * dh, :] = _gate_row(gates, h, 0) * o_c[:, hs(h)]


def _nsa_attn_kernel(qt_ref, gt_ref, gb_ref, oc_ref, neg_ref, ks_ref, vst_ref, kw_ref, vwt_ref, o_ref,
                     qext_ref, s_ref):
    qb, dh, hpg, sb = NSA_QBLOCK, NSA_DH, NSA_HPG, NSA_SEL_BLOCK
    assert NSA_KV_GROUPS == 2
    groups = range(NSA_KV_GROUPS)
    qrows = hpg * dh
    n = pl.program_id(1)
    t0 = n * qb
    seq = kw_ref.shape[1]
    hs = lambda h: slice(h * qb, (h + 1) * qb)
    q = [_heads_on_lanes(qt_ref, g * qrows) for g in groups]
    t = t0 + lax.broadcasted_iota(jnp.int32, (1, qb), 1)

    tk = SEL_KV_TILE
    bpt = tk // sb
    mrows = 16
    for g in groups:
        qext_ref[g, 0:dh, :] = q[g]
        qext_ref[g, dh + mrows:, :] = jnp.zeros((qext_ref.shape[1] - dh - mrows, hpg * qb), BF16)
    kpos_tile = lax.broadcasted_iota(jnp.int32, (tk, 1), 0)

    def scores(g, kt):
        rows = neg_ref[g, pl.ds(pl.multiple_of(kt * bpt, bpt), bpt), :]
        ext = jnp.concatenate([rows, jnp.zeros((mrows - bpt, qb), F32)], axis=0).astype(BF16)
        qext_ref[g, dh:dh + mrows, :] = jnp.concatenate([ext] * hpg, axis=1)
        s_ref[g] = _dot(ks_ref[g, pl.ds(pl.multiple_of(kt * tk, tk), tk), :], qext_ref[g])

    def accumulate(g, kt, carry_g, diagonal):
        m, acc = carry_g
        k0 = pl.multiple_of(kt * tk, tk)
        if diagonal:
            ok = kpos_tile + k0 <= t
        m_new, p = [], []
        for h in range(hpg):
            s_h = s_ref[g, :, hs(h)]
            if diagonal:
                s_h = jnp.where(ok, s_h, NEG_BIG)
            m_h = jnp.maximum(m[:, hs(h)], jnp.max(s_h, axis=0, keepdims=True))
            p.append(jnp.exp2(s_h - m_h).astype(BF16))
            m_new.append(m_h)
        m_new = jnp.concatenate(m_new, axis=1)
        acc = jnp.exp2(m - m_new) * acc + _dot(vst_ref[g, :, pl.ds(k0, tk)], jnp.concatenate(p, axis=1))
        return m_new, acc

    def sel_step(kt, carry):
        scores(0, kt)
        c1 = accumulate(1, kt, carry[1], False)
        scores(1, kt + 1)
        c0 = accumulate(0, kt, carry[0], False)
        return c0, c1

    last = (n + tk // qb) // (tk // qb) - 1
    init = (jnp.full((1, hpg * qb), NEG_BIG, F32), jnp.zeros((NSA_VROWS, hpg * qb), F32))
    scores(1, 0)
    carry = lax.fori_loop(0, last, sel_step, (init, init))
    scores(0, last)
    carry = (carry[0], accumulate(1, last, carry[1], True))
    carry = (accumulate(0, last, carry[0], True), carry[1])

    wk = min(NSA_WINDOW + qb, seq)
    start = pl.multiple_of(jnp.maximum(t0 + qb - wk, 0), qb)
    kpos = start + lax.broadcasted_iota(jnp.int32, (wk, 1), 0)
    ok_w = (kpos <= t) & (kpos > t - NSA_WINDOW)
    gates = jax.nn.sigmoid(gt_ref[...] + gb_ref[...])
    for g in groups:
        acc_s = carry[g][1]
        o_s = acc_s[:dh] / jnp.maximum(acc_s[dh:dh + 1], 1e-30)
        s_w = _dot(kw_ref[g, pl.ds(start, wk), :], q[g])
        p_w = []
        for h in range(hpg):
            s_h = jnp.where(ok_w, s_w[:, hs(h)], NEG_BIG)
            p_w.append(jnp.exp2(s_h - jnp.max(s_h, axis=0, keepdims=True)).astype(BF16))
        acc_w = _dot(vwt_ref[g, :, pl.ds(start, wk)], jnp.concatenate(p_w, axis=1))
        o_w = acc_w[:dh] / jnp.maximum(acc_w[dh:dh + 1], 1e-30)
        gates_g = gates[g * NSA_GATE_ROWS:(g + 1) * NSA_GATE_ROWS]
        for h in range(hpg):
            r = slice(g * qrows + h * dh, g * qrows + (h + 1) * dh)
            o_ref[r, :] = (oc_ref[r, :] + _gate_row(gates_g, h, 1) * o_s[:, hs(h)]
                           + _gate_row(gates_g, h, 2) * o_w[:, hs(h)])


def _nsa_attn(qt, ngt, gate_b, kcmp, vcmpt, ks, vst, kw, vwt):
    bsz, g_n, seq, dh = kw.shape
    nr = kcmp.shape[2]
    n_sel = seq // NSA_SEL_BLOCK
    n_qb = seq // NSA_QBLOCK
    n_cmp = (seq - NSA_CMP_BLOCK) // NSA_CMP_STRIDE + 1
    cmp_start = jnp.arange(nr) * NSA_CMP_STRIDE
    sel_start = jnp.arange(n_sel) * NSA_SEL_BLOCK
    ovl = ((cmp_start[None, :] < sel_start[:, None] + NSA_SEL_BLOCK)
           & (cmp_start[None, :] + NSA_CMP_BLOCK > sel_start[:, None])
           & (jnp.arange(nr)[None, :] < n_cmp)).astype(BF16)
    gb = jnp.pad(gate_b.reshape(g_n, NSA_HPG * NSA_BRANCHES), ((0, 0), (0, NSA_GATE_ROWS - NSA_HPG * NSA_BRANCHES)))
    gb = gb.reshape(g_n, NSA_GATE_ROWS, 1)
    qrows = NSA_HPG * dh
    full = lambda a, c: pl.BlockSpec((None, None, a, c), lambda b, g, n: (b, g, 0, 0))
    cols = lambda r, w: pl.BlockSpec((None, r, w), lambda b, g, n: (b, g, n))
    gb_spec = pl.BlockSpec((None, NSA_GATE_ROWS, 1), lambda b, g, n: (g, 0, 0))
    sw = min(SEL_QBLOCKS * NSA_QBLOCK, seq)
    oc, neg = pl.pallas_call(
        _nsa_select_kernel,
        grid=(bsz, g_n, seq // sw),
        in_specs=[cols(qrows, sw), cols(NSA_GATE_ROWS, sw), gb_spec,
                  full(nr, dh), full(dh, nr), _const_spec((n_sel, nr))],
        out_specs=(cols(qrows, sw), pl.BlockSpec((None, None, n_sel, sw), lambda b, g, n: (b, g, 0, n))),
        out_shape=(jax.ShapeDtypeStruct((bsz, g_n * qrows, seq), F32),
                   jax.ShapeDtypeStruct((bsz, g_n, n_sel, seq), F32)),
        compiler_params=_cparams("parallel", "parallel", "parallel"),
        name="nsa_select",
    )(qt, ngt, gb, kcmp, vcmpt, ovl)
    qb = NSA_QBLOCK
    qcols = lambda r: pl.BlockSpec((None, r, qb), lambda b, n: (b, 0, n))
    whole = lambda a, c: pl.BlockSpec((None, g_n, a, c), lambda b, n: (b, 0, 0, 0))
    return pl.pallas_call(
        _nsa_attn_kernel,
        grid=(bsz, n_qb),
        in_specs=[qcols(g_n * qrows), qcols(g_n * NSA_GATE_ROWS), _const_spec((g_n * NSA_GATE_ROWS, 1)),
                  qcols(g_n * qrows), pl.BlockSpec((None, g_n, n_sel, qb), lambda b, n: (b, 0, 0, n)),
                  whole(seq, LANE), whole(NSA_VROWS, seq), whole(seq, dh), whole(NSA_VROWS, seq)],
        out_specs=qcols(g_n * qrows),
        out_shape=jax.ShapeDtypeStruct((bsz, g_n * qrows, seq), F32),
        scratch_shapes=[pltpu.VMEM((g_n, LANE, NSA_HPG * qb), BF16),
                        pltpu.VMEM((g_n, SEL_KV_TILE, NSA_HPG * qb), F32)],
        compiler_params=_cparams("parallel", "parallel"),
        name="nsa_attn",
    )(qt, ngt, gb.reshape(g_n * NSA_GATE_ROWS, 1), oc, neg, ks, vst, kw, vwt)


def _even_out_kernel(h_ref, oa_ref, obt_ref, w_ref, g_ref, o_ref):
    m = _dot(oa_ref[...].astype(BF16), w_ref[:GLA_V, :]) + lax.dot_general(
        obt_ref[...].astype(BF16), w_ref[GLA_V:, :], (((0,), (0,)), ((), ())), preferred_element_type=F32)
    o_ref[...] = h_ref[...] + _rms(m, g_ref[...])


def _even_out(h, o_a, o_bt, w_out, g):
    bsz, seq, d = h.shape
    tm = min(ROW_TILE, seq)
    row = lambda w: pl.BlockSpec((None, tm, w), lambda b, i: (b, i, 0))
    return pl.pallas_call(
        _even_out_kernel,
        grid=(bsz, seq // tm),
        in_specs=[row(d), row(GLA_V), pl.BlockSpec((None, NSA_Q, tm), lambda b, i: (b, 0, i)),
                  _const_spec(w_out.shape), _const_spec((1, d))],
        out_specs=row(d),
        out_shape=jax.ShapeDtypeStruct(h.shape, F32),
        compiler_params=_cparams("parallel", "parallel"),
        name="even_out",
    )(h, o_a, o_bt, w_out.astype(BF16), g.reshape(1, d))


def _ffn_kernel(h_ref, g_in_ref, w1_ref, w2_ref, g_out_ref, o_ref):
    h = h_ref[...]
    xn = _rms(h, g_in_ref[...]).astype(BF16)
    hid = w1_ref.shape[1]
    step = 1024
    acc = jnp.zeros(h.shape, F32)
    for j in range(0, hid, step):
        a = jnp.maximum(_dot(xn, w1_ref[:, j:j + step]), 0.0)
        acc = acc + _dot((a * a).astype(BF16), w2_ref[j:j + step, :])
    o_ref[...] = h + _rms(acc, g_out_ref[...])


def _ffn(h, g_in, w1, w2, g_out):
    bsz, seq, d = h.shape
    tm = min(ROW_TILE, seq)
    row = pl.BlockSpec((None, tm, d), lambda b, i: (b, i, 0))
    return pl.pallas_call(
        _ffn_kernel,
        grid=(bsz, seq // tm),
        in_specs=[row, _const_spec((1, d)), _const_spec(w1.shape), _const_spec(w2.shape), _const_spec((1, d))],
        out_specs=row,
        out_shape=jax.ShapeDtypeStruct(h.shape, F32),
        compiler_params=_cparams("parallel", "parallel"),
        name="ffn",
    )(h, g_in.reshape(1, d), w1.astype(BF16), w2.astype(BF16), g_out.reshape(1, d))


def _sgu_kernel(h_ref, g_in_ref, w_in_ref, ln_g_ref, ln_b_ref, ws_ref, bs_ref, w_out_ref, g_out_ref, o_ref):
    e, c, gw = SGU_WIDTH, SGU_CHUNK, SGU_WIDTH // SGU_GROUPS
    h = h_ref[...]
    tm = h.shape[0]
    xn = _rms(h, g_in_ref[...]).astype(BF16)
    u = jax.nn.gelu(_dot(xn, w_in_ref[:, :e]))
    v = jax.nn.gelu(_dot(xn, w_in_ref[:, e:]))
    mu = jnp.mean(v, axis=-1, keepdims=True)
    var = jnp.mean((v - mu) ** 2, axis=-1, keepdims=True)
    vn = ((v - mu) * lax.rsqrt(var + NORM_EPS) * ln_g_ref[...] + ln_b_ref[...]).astype(BF16)
    causal = lax.broadcasted_iota(jnp.int32, (c, c), 0) >= lax.broadcasted_iota(jnp.int32, (c, c), 1)
    w_c = [jnp.where(causal, ws_ref[g], 0.0).astype(BF16) for g in range(SGU_GROUPS)]
    rows = []
    for r0 in range(0, tm, c):
        mixed = [_dot(w_c[g], vn[r0:r0 + c, g * gw:(g + 1) * gw]) for g in range(SGU_GROUPS)]
        rows.append(jnp.concatenate(mixed, axis=1) + bs_ref[...])
    y = (u * jnp.concatenate(rows, axis=0)).astype(BF16)
    o_ref[...] = h + _rms(_dot(y, w_out_ref[...]), g_out_ref[...])


def _sgu(h, g_in, w_in, ln_g, ln_b, w_s, b_s, w_out, g_out):
    bsz, seq, d = h.shape
    tm = min(SGU_ROW_TILE, seq)
    e = SGU_WIDTH
    bias = jnp.repeat(b_s.T, e // SGU_GROUPS, axis=1)
    row = pl.BlockSpec((None, tm, d), lambda b, i: (b, i, 0))
    return pl.pallas_call(
        _sgu_kernel,
        grid=(bsz, seq // tm),
        in_specs=[row, _const_spec((1, d)), _const_spec(w_in.shape), _const_spec((1, e)), _const_spec((1, e)),
                  _const_spec(w_s.shape), _const_spec(bias.shape), _const_spec(w_out.shape), _const_spec((1, d))],
        out_specs=row,
        out_shape=jax.ShapeDtypeStruct(h.shape, F32),
        compiler_params=_cparams("parallel", "parallel"),
        name="sgu",
    )(h, g_in.reshape(1, d), w_in.astype(BF16), ln_g.reshape(1, e), ln_b.reshape(1, e), w_s, bias,
      w_out.astype(BF16), g_out.reshape(1, d))


def _even_mixer(h, g_in, g_out, w_in, w_out, w_gate, b_gate, gla_norm, gate_b, cmp_pos, cmp_w1, cmp_w2):
    bsz, seq, _ = h.shape
    (gq, gk, gv, gvt, gr, glog, kc, vc, ks, kw, qt, vst, vwt, ngt) = _even_proj(h, g_in, w_in, w_gate, b_gate)
    o_a = _gla(gq, gk, gv, gvt, gr, glog, gla_norm)
    kcmp, vcmpt = _nsa_compress(kc, vc, cmp_pos, cmp_w1, cmp_w2)
    o_bt = _nsa_attn(qt, ngt, gate_b, kcmp, vcmpt, ks, vst, kw, vwt)
    return _even_out(h, o_a, o_bt, w_out, g_out)


def kernel(x, norm_g, ffn_w1, ffn_w2, e_w_in, e_w_out, gla_w_gate, gla_b_gate, gla_norm, nsa_gate_b, nsa_cmp_pos, nsa_cmp_w1, nsa_cmp_w2, o_w_in, o_ln_g, o_ln_b, o_w_s, o_b_s, o_w_out):
    h = x
    depth = norm_g.shape[0]
    for layer in range(depth):
        i = layer // 2
        if layer % 2 == 0:
            h = _even_mixer(h, norm_g[layer, 0], norm_g[layer, 1], e_w_in[i], e_w_out[i], gla_w_gate[i],
                            gla_b_gate[i], gla_norm[i], nsa_gate_b[i], nsa_cmp_pos[i], nsa_cmp_w1[i], nsa_cmp_w2[i])
        else:
            h = _sgu(h, norm_g[layer, 0], o_w_in[i], o_ln_g[i], o_ln_b[i], o_w_s[i], o_b_s[i], o_w_out[i],
                     norm_g[layer, 1])
        h = _ffn(h, norm_g[layer, 2], ffn_w1[layer], ffn_w2[layer], norm_g[layer, 3])
    return h
```

```python
import functools

import jax
import jax.numpy as jnp
from jax import lax
from jax.experimental import pallas as pl
from jax.experimental.pallas import tpu as pltpu

F32 = jnp.float32
BF16 = jnp.bfloat16

D_MODEL = 1024
GLA_HEADS = 4
GLA_DK = 64
GLA_DV = 128
GLA_GATE_RANK = 16
GLA_GATE_TEMP = 16.0
GLA_CHUNK = 64
NSA_HEADS = 8
NSA_KV_GROUPS = 2
NSA_HPG = NSA_HEADS // NSA_KV_GROUPS
NSA_DH = 64
NSA_CMP_BLOCK = 32
NSA_CMP_STRIDE = 16
NSA_CMP_HIDDEN = 128
NSA_SEL_BLOCK = 64
NSA_TOPN = 16
NSA_WINDOW = 512
NSA_QBLOCK = 128
NSA_BRANCHES = 3
SGU_WIDTH = 2 * D_MODEL
SGU_GROUPS = 8
SGU_CHUNK = 128
FFN_HIDDEN = 4 * D_MODEL
NORM_EPS = 1e-6
NEG_BIG = -1e30
POS_BIG = 1e30

GLA_QK = GLA_HEADS * GLA_DK
GLA_V = GLA_HEADS * GLA_DV
NSA_Q = NSA_HEADS * NSA_DH
NSA_KV = NSA_KV_GROUPS * NSA_DH
NSA_GATE_ROWS = 16
NSA_VROWS = NSA_DH + 16
LOG2E = 1.4426950408889634
LANE = 128

VMEM_LIMIT_BYTES = 56 * 1024 * 1024

ROW_TILE = 512
SGU_ROW_TILE = 512
GLA_ROW_TILE = 256
SEL_KV_TILE = 512
SEL_VARIANTS = 4
SEL_QBLOCKS = 4


def _cparams(*sem):
    return pltpu.CompilerParams(dimension_semantics=sem, vmem_limit_bytes=VMEM_LIMIT_BYTES)


def _rms(x, g):
    return x * lax.rsqrt(jnp.mean(x * x, axis=-1, keepdims=True) + NORM_EPS) * g


def _dot(a, b):
    return jnp.dot(a, b, preferred_element_type=F32)


def _dot_nt(a, b):
    return lax.dot_general(a, b, (((1,), (1,)), ((), ())), preferred_element_type=F32)


def _split3(x):
    x1 = x.astype(BF16)
    r1 = x - x1.astype(F32)
    x2 = r1.astype(BF16)
    r2 = r1 - x2.astype(F32)
    return x1, x2, r2.astype(BF16)


def _dot_exact_lhs(a_bf16, b_f32):
    b1, b2, b3 = _split3(b_f32)
    return _dot(a_bf16, b1) + _dot(a_bf16, b2) + _dot(a_bf16, b3)


def _const_spec(shape):
    return pl.BlockSpec(shape, lambda *_: (0,) * len(shape))


def _even_proj_kernel(h_ref, g_ref, wrow_ref, wt_ref, wlr_ref, wgate_ref, bgate_ref,
                      gq_ref, gk_ref, gv_ref, gvt_ref, gr_ref, glog_ref,
                      kc_ref, vc_ref, ks_ref, kw_ref, qt_ref, vst_ref, vwt_ref, ngt_ref):
    xn = _rms(h_ref[...], g_ref[...]).astype(BF16)
    o = 0
    gq_ref[...] = _dot(xn, wrow_ref[:, o:o + GLA_QK]) * (GLA_DK ** -0.5)
    o += GLA_QK
    gk_ref[...] = _dot(xn, wrow_ref[:, o:o + GLA_QK])
    o += GLA_QK
    gv_ref[...] = _dot(xn, wrow_ref[:, o:o + GLA_V]).astype(BF16)
    o += GLA_V
    gr_ref[...] = _dot(xn, wrow_ref[:, o:o + GLA_V])
    o += GLA_V
    tm = xn.shape[0]
    for ref in (kc_ref, vc_ref, kw_ref):
        y = _dot(xn, wrow_ref[:, o:o + NSA_KV]).astype(BF16)
        for g in range(NSA_KV_GROUPS):
            ref[g] = y[:, g * NSA_DH:(g + 1) * NSA_DH]
        o += NSA_KV
    y = _dot(xn, wrow_ref[:, o:o + NSA_KV])
    lane = lax.broadcasted_iota(jnp.int32, (tm, NSA_DH), 1)
    blk = lax.broadcasted_iota(jnp.int32, (tm, NSA_DH), 0) // NSA_SEL_BLOCK
    onehot = jnp.where(lane == blk % (SEL_KV_TILE // NSA_SEL_BLOCK), 1.0, 0.0)
    for g in range(NSA_KV_GROUPS):
        ks_ref[g] = jnp.concatenate([y[:, g * NSA_DH:(g + 1) * NSA_DH], onehot], axis=1).astype(BF16)
    o = 0
    gvt_ref[...] = _dot_nt(wt_ref[o:o + GLA_V, :], xn).astype(BF16)
    o += GLA_V
    qt_ref[...] = (_dot_nt(wt_ref[o:o + NSA_Q, :], xn) * (NSA_DH ** -0.5 * LOG2E)).astype(BF16)
    o += NSA_Q
    ones_row = jnp.where(lax.broadcasted_iota(jnp.int32, (NSA_VROWS - NSA_DH, tm), 0) == 0, 1.0, 0.0)
    for ref in (vst_ref, vwt_ref):
        y = _dot_nt(wt_ref[o:o + NSA_KV, :], xn)
        for g in range(NSA_KV_GROUPS):
            ref[g] = jnp.concatenate([y[g * NSA_DH:(g + 1) * NSA_DH], ones_row], axis=0).astype(BF16)
        o += NSA_KV
    ngt_ref[...] = _dot_nt(wt_ref[o:o + 2 * NSA_GATE_ROWS, :], xn)
    lr = _dot(xn, wlr_ref[...])
    l1, l2, l3 = _split3(lr)
    w1, w2, w3 = _split3(wgate_ref[...])
    z = (_dot(l1, w1) + _dot(l1, w2) + _dot(l2, w1) + _dot(l2, w2) + _dot(l1, w3) + _dot(l3, w1)
         + bgate_ref[...])
    glog_ref[...] = jax.nn.log_sigmoid(z) * (1.0 / GLA_GATE_TEMP)


def _even_proj(h, g, w_in, w_gate, b_gate):
    bsz, seq, d = h.shape
    tm = min(ROW_TILE, seq)
    sizes = (GLA_QK, GLA_QK, GLA_V, GLA_GATE_RANK, GLA_V, NSA_Q) + (NSA_KV,) * 6 + (NSA_HEADS * NSA_BRANCHES,)
    offs = [0]
    for s in sizes:
        offs.append(offs[-1] + s)
    col = lambda i: w_in[:, offs[i]:offs[i + 1]]
    gq, gk, gv, glr, gr, nq, kc, vc, ks, vs, kw, vw, ng = [col(i) for i in range(13)]
    wrow = jnp.concatenate([gq, gk, gv, gr, kc, vc, kw, ks], axis=1).astype(BF16)
    ng_g = ng.reshape(d, NSA_KV_GROUPS, NSA_HPG * NSA_BRANCHES)
    ng_g = jnp.pad(ng_g, ((0, 0), (0, 0), (0, NSA_GATE_ROWS - NSA_HPG * NSA_BRANCHES)))
    wt = jnp.concatenate([gv, nq, vs, vw, ng_g.reshape(d, NSA_KV_GROUPS * NSA_GATE_ROWS)], axis=1).T.astype(BF16)
    wlr = jnp.pad(glr, ((0, 0), (0, LANE - GLA_GATE_RANK))).astype(BF16)
    wgate = jnp.pad(w_gate, ((0, LANE - GLA_GATE_RANK), (0, 0)))
    n_t = wt.shape[0]

    row = lambda w: pl.BlockSpec((None, tm, w), lambda b, i: (b, i, 0))
    grp = pl.BlockSpec((None, NSA_KV_GROUPS, tm, NSA_DH), lambda b, i: (b, 0, i, 0))
    tr = lambda w: pl.BlockSpec((None, w, tm), lambda b, i: (b, 0, i))
    grp_ext = pl.BlockSpec((None, NSA_KV_GROUPS, tm, LANE), lambda b, i: (b, 0, i, 0))
    grp_tr = pl.BlockSpec((None, NSA_KV_GROUPS, NSA_VROWS, tm), lambda b, i: (b, 0, 0, i))
    sds = jax.ShapeDtypeStruct
    kv_shape = sds((bsz, NSA_KV_GROUPS, seq, NSA_DH), BF16)
    vt_shape = sds((bsz, NSA_KV_GROUPS, NSA_VROWS, seq), BF16)
    out_shape = (
        sds((bsz, seq, GLA_QK), F32), sds((bsz, seq, GLA_QK), F32), sds((bsz, seq, GLA_V), BF16),
        sds((bsz, GLA_V, seq), BF16), sds((bsz, seq, GLA_V), F32), sds((bsz, seq, GLA_QK), F32),
        kv_shape, kv_shape, sds((bsz, NSA_KV_GROUPS, seq, LANE), BF16), kv_shape,
        sds((bsz, NSA_Q, seq), BF16), vt_shape, vt_shape,
        sds((bsz, NSA_KV_GROUPS * NSA_GATE_ROWS, seq), F32),
    )
    out_specs = (
        row(GLA_QK), row(GLA_QK), row(GLA_V), tr(GLA_V), row(GLA_V), row(GLA_QK),
        grp, grp, grp_ext, grp,
        tr(NSA_Q), grp_tr, grp_tr, tr(NSA_KV_GROUPS * NSA_GATE_ROWS),
    )
    return pl.pallas_call(
        _even_proj_kernel,
        grid=(bsz, seq // tm),
        in_specs=[
            row(d), _const_spec((1, d)), _const_spec(wrow.shape), _const_spec((n_t, d)),
            _const_spec((d, LANE)), _const_spec((LANE, GLA_QK)), _const_spec((1, GLA_QK)),
        ],
        out_specs=out_specs,
        out_shape=out_shape,
        compiler_params=_cparams("parallel", "parallel"),
        name="even_proj",
    )(h, g.reshape(1, d), wrow, wt, wlr, wgate, b_gate.reshape(1, GLA_QK))


def _gla_kernel(q_ref, k_ref, v_ref, vt_ref, r_ref, glog_ref, ng_ref, o_ref, state_ref):
    c = GLA_CHUNK
    ts = q_ref.shape[0]

    @pl.when(pl.program_id(1) == 0)
    def _():
        state_ref[...] = jnp.zeros_like(state_ref)

    ri = lax.broadcasted_iota(jnp.int32, (c, c), 0)
    ci = lax.broadcasted_iota(jnp.int32, (c, c), 1)
    causal = ri >= ci
    tril = jnp.where(causal, 1.0, 0.0).astype(BF16)
    row2 = lax.broadcasted_iota(jnp.int32, (2 * c, 1), 0)
    state = [state_ref[h] for h in range(GLA_HEADS)]

    for pair in range(ts // (2 * c)):
        p0 = pair * 2 * c
        vt_pair = vt_ref[:, p0:p0 + 2 * c]
        for half in range(2):
            r0 = p0 + half * c
            gl = glog_ref[r0:r0 + c, :]
            b = _dot_exact_lhs(tril, gl)
            b_last = b[c - 1:c, :]
            q_t = (q_ref[r0:r0 + c, :] * jnp.exp(b)).astype(BF16)
            k = k_ref[r0:r0 + c, :]
            k_t = (k * jnp.exp(-b)).astype(BF16)
            k_end = (k * jnp.exp(b_last - b)).astype(BF16)
            decay = jnp.exp(b_last)
            in_half = (row2 >= half * c) & (row2 < (half + 1) * c)
            for h in range(GLA_HEADS):
                ks = slice(h * GLA_DK, (h + 1) * GLA_DK)
                vs = slice(h * GLA_DV, (h + 1) * GLA_DV)
                a = jnp.where(causal, _dot_nt(q_t[:, ks], k_t[:, ks]), 0.0).astype(BF16)
                o = _dot(a, v_ref[r0:r0 + c, vs]) + _dot_nt(q_t[:, ks], state[h].astype(BF16))
                ke_pair = jnp.where(in_half, jnp.concatenate([k_end[:, ks], k_end[:, ks]], axis=0), 0.0)
                state[h] = state[h] * decay[:, ks] + _dot(vt_pair[vs, :], ke_pair.astype(BF16))
                o = o * lax.rsqrt(jnp.mean(o * o, axis=-1, keepdims=True) + NORM_EPS) * ng_ref[:, vs]
                o_ref[r0:r0 + c, vs] = o * jax.nn.silu(r_ref[r0:r0 + c, vs])
    for h in range(GLA_HEADS):
        state_ref[h] = state[h]


def _gla(gq, gk, gv, gvt, gr, glog, gla_norm):
    bsz, seq, _ = gq.shape
    ts = min(GLA_ROW_TILE, seq)
    row = lambda w: pl.BlockSpec((None, ts, w), lambda b, i: (b, i, 0))
    return pl.pallas_call(
        _gla_kernel,
        grid=(bsz, seq // ts),
        in_specs=[row(GLA_QK), row(GLA_QK), row(GLA_V),
                  pl.BlockSpec((None, GLA_V, ts), lambda b, i: (b, 0, i)),
                  row(GLA_V), row(GLA_QK), _const_spec((1, GLA_V))],
        out_specs=row(GLA_V),
        out_shape=jax.ShapeDtypeStruct((bsz, seq, GLA_V), F32),
        scratch_shapes=[pltpu.VMEM((GLA_HEADS, GLA_DV, GLA_DK), F32)],
        compiler_params=_cparams("parallel", "arbitrary"),
        name="gla",
    )(gq, gk, gv, gvt, gr, glog, gla_norm.reshape(1, GLA_V))


def _nsa_compress_kernel(kc_ref, vc_ref, pos_ref, w1_ref, w2_ref, w2t_ref, kcmp_ref, vcmpt_ref):
    half = (NSA_CMP_BLOCK // 2) * NSA_DH

    def hidden(x_ref, i):
        x = x_ref[...]
        first = _dot(x, w1_ref[i, :half, :])
        second = _dot(x, w1_ref[i, half:, :])
        nr = first.shape[0]
        nxt = jnp.concatenate([second[1:], second[:1]], axis=0)
        posc = _dot(pos_ref[i], w1_ref[i])[0:1]
        return jax.nn.gelu(first + nxt + posc).astype(BF16)

    kcmp_ref[...] = _dot(hidden(kc_ref, 0), w2_ref[0]).astype(BF16)
    vcmpt_ref[...] = _dot_nt(w2t_ref[...], hidden(vc_ref, 1)).astype(BF16)


def _nsa_compress(kc, vc, cmp_pos, cmp_w1, cmp_w2):
    bsz, g_n, seq, dh = kc.shape
    grp_tok = NSA_CMP_BLOCK // 2
    nr = seq // grp_tok
    feat = grp_tok * dh
    kc_r = kc.reshape(bsz, g_n, nr, feat)
    vc_r = vc.reshape(bsz, g_n, nr, feat)
    pos = jnp.broadcast_to(cmp_pos.reshape(2, 1, NSA_CMP_BLOCK * dh), (2, 8, NSA_CMP_BLOCK * dh)).astype(BF16)
    blk = pl.BlockSpec((None, None, nr, feat), lambda b, g: (b, g, 0, 0))
    return pl.pallas_call(
        _nsa_compress_kernel,
        grid=(bsz, g_n),
        in_specs=[blk, blk, _const_spec(pos.shape), _const_spec(cmp_w1.shape), _const_spec(cmp_w2.shape),
                  _const_spec((dh, NSA_CMP_HIDDEN))],
        out_specs=(pl.BlockSpec((None, None, nr, dh), lambda b, g: (b, g, 0, 0)),
                   pl.BlockSpec((None, None, dh, nr), lambda b, g: (b, g, 0, 0))),
        out_shape=(jax.ShapeDtypeStruct((bsz, g_n, nr, dh), BF16),
                   jax.ShapeDtypeStruct((bsz, g_n, dh, nr), BF16)),
        compiler_params=_cparams("parallel", "parallel"),
        name="nsa_compress",
    )(kc_r, vc_r, pos, cmp_w1.astype(BF16), cmp_w2.astype(BF16), cmp_w2[1].T.astype(BF16))


def _heads_on_lanes(qt_ref, row0=0):
    return jnp.concatenate([qt_ref[row0 + h * NSA_DH:row0 + (h + 1) * NSA_DH, :] for h in range(NSA_HPG)], axis=1)


def _gate_row(gates, h, branch):
    r = h * NSA_BRANCHES + branch
    return gates[r:r + 1, :]


def _nsa_select_kernel(qt_ref, gt_ref, gb_ref, kcmp_ref, vcmpt_ref, ovl_ref, oc_ref, neg_ref):
    dh, hpg = NSA_DH, NSA_HPG
    w = qt_ref.shape[1]
    nr = kcmp_ref.shape[0]
    n_sel = neg_ref.shape[0]
    step = pl.program_id(2)
    hs = lambda h: slice(h * w, (h + 1) * w)

    def select(ns):
        cr = ns * (nr // n_sel)
        q = _heads_on_lanes(qt_ref)
        t = step * w + lax.broadcasted_iota(jnp.int32, (1, w), 1)

        s_c = _dot(kcmp_ref[:cr, :], q)
        cmp_end = lax.broadcasted_iota(jnp.int32, (cr, 1), 0) * NSA_CMP_STRIDE + (NSA_CMP_BLOCK - 1)
        ok_c = cmp_end <= t
        has_key = t >= NSA_CMP_BLOCK - 1
        p_c = []
        for h in range(hpg):
            s_h = jnp.where(ok_c, s_c[:, hs(h)], NEG_BIG)
            e = jnp.exp2(s_h - jnp.max(s_h, axis=0, keepdims=True))
            p_c.append(e * jnp.where(has_key, 1.0 / jnp.sum(e, axis=0, keepdims=True), 0.0))
        o_c = _dot(vcmpt_ref[:, :cr], jnp.concatenate(p_c, axis=1).astype(BF16))

        p1, p2, _ = _split3(sum(p_c[1:], p_c[0]))
        imp = _dot(ovl_ref[:ns, :cr], p1) + _dot(ovl_ref[:ns, :cr], p2)
        j = lax.broadcasted_iota(jnp.int32, (ns, 1), 0)
        jf = j.astype(F32)
        cur = t // NSA_SEL_BLOCK
        forced = (j == 0) | (j == cur) | (j == cur - 1)
        score = jnp.where(forced, POS_BIG, jnp.where(j <= cur, imp, NEG_BIG))
        for _ in range(min(NSA_TOPN, ns)):
            top = jnp.max(score, axis=0, keepdims=True)
            first = jnp.min(jnp.where(score == top, jf, float(ns)), axis=0, keepdims=True)
            score = jnp.where(jf == first, -jnp.inf, score)
        neg_ref[:ns, :] = jnp.where((score == -jnp.inf) & (j <= cur), 0.0, NEG_BIG)
        if ns < n_sel:
            neg_ref[ns:, :] = jnp.full((n_sel - ns, w), NEG_BIG, F32)

        gates = jax.nn.sigmoid(gt_ref[...] + gb_ref[...])
        for h in range(hpg):
            oc_ref[h * dh:(h + 1) * dh, :] = _gate_row(gates, h, 0) * o_c[:, hs(h)]

    n_var = max(1, min(SEL_VARIANTS, n_sel // NSA_TOPN))
    blocks_per_step = w // NSA_SEL_BLOCK
    lo = 0
    for v in range(1, n_var + 1):
        ns = n_sel * v // n_var
        hi = ns // blocks_per_step
        pl.when((step >= lo) & (step < hi))(functools.partial(select, ns))
        lo = hi


def _nsa_attn_kernel(qt_ref, gt_ref, gb_ref, oc_ref, neg_ref, ks_ref, vst_ref, kw_ref, vwt_ref, o_ref,
                     qext_ref, s_ref, sw_ref):
    qb, dh, hpg, sb = NSA_QBLOCK, NSA_DH, NSA_HPG, NSA_SEL_BLOCK
    assert NSA_KV_GROUPS == 2
    groups = range(NSA_KV_GROUPS)
    qrows = hpg * dh
    n = pl.program_id(1)
    t0 = n * qb
    seq = kw_ref.shape[1]
    hs = lambda h: slice(h * qb, (h + 1) * qb)
    q = [_heads_on_lanes(qt_ref, g * qrows) for g in groups]
    t = t0 + lax.broadcasted_iota(jnp.int32, (1, qb), 1)

    tk = SEL_KV_TILE
    bpt = tk // sb
    mrows = 16
    for g in groups:
        qext_ref[g, 0:dh, :] = q[g]
        qext_ref[g, dh + mrows:, :] = jnp.zeros((qext_ref.shape[1] - dh - mrows, hpg * qb), BF16)
    kpos_tile = lax.broadcasted_iota(jnp.int32, (tk, 1), 0)

    def scores(g, kt):
        rows = neg_ref[g, pl.ds(pl.multiple_of(kt * bpt, bpt), bpt), :]
        ext = jnp.concatenate([rows, jnp.zeros((mrows - bpt, qb), F32)], axis=0).astype(BF16)
        qext_ref[g, dh:dh + mrows, :] = jnp.concatenate([ext] * hpg, axis=1)
        s_ref[g] = _dot(ks_ref[g, pl.ds(pl.multiple_of(kt * tk, tk), tk), :], qext_ref[g])

    def accumulate(g, kt, carry_g, diagonal):
        m, acc = carry_g
        k0 = pl.multiple_of(kt * tk, tk)
        if diagonal:
            ok = kpos_tile + k0 <= t
        m_new, p = [], []
        for h in range(hpg):
            s_h = s_ref[g, :, hs(h)]
            if diagonal:
                s_h = jnp.where(ok, s_h, NEG_BIG)
            m_h = jnp.maximum(m[:, hs(h)], jnp.max(s_h, axis=0, keepdims=True))
            p.append(jnp.exp2(s_h - m_h).astype(BF16))
            m_new.append(m_h)
        m_new = jnp.concatenate(m_new, axis=1)
        acc = jnp.exp2(m - m_new) * acc + _dot(vst_ref[g, :, pl.ds(k0, tk)], jnp.concatenate(p, axis=1))
        return m_new, acc

    def sel_step(kt, carry):
        scores(0, kt)
        c1 = accumulate(1, kt, carry[1], False)
        scores(1, kt + 1)
        c0 = accumulate(0, kt, carry[0], False)
        return c0, c1

    wk = sw_ref.shape[1]
    start = pl.multiple_of(jnp.maximum(t0 + qb - wk, 0), qb)
    kpos = start + lax.broadcasted_iota(jnp.int32, (wk, 1), 0)
    ok_w = (kpos <= t) & (kpos > t - NSA_WINDOW)

    def window_scores(g):
        sw_ref[g] = _dot(kw_ref[g, pl.ds(start, wk), :], q[g])

    def window_out(g):
        p_w = []
        for h in range(hpg):
            s_h = jnp.where(ok_w, sw_ref[g, :, hs(h)], NEG_BIG)
            p_w.append(jnp.exp2(s_h - jnp.max(s_h, axis=0, keepdims=True)).astype(BF16))
        acc_w = _dot(vwt_ref[g, :, pl.ds(start, wk)], jnp.concatenate(p_w, axis=1))
        return acc_w[:dh] / jnp.maximum(acc_w[dh:dh + 1], 1e-30)

    last = (n + tk // qb) // (tk // qb) - 1
    init = (jnp.full((1, hpg * qb), NEG_BIG, F32), jnp.zeros((NSA_VROWS, hpg * qb), F32))
    window_scores(0)
    scores(1, 0)
    o_w = [window_out(0)]
    window_scores(1)
    carry = lax.fori_loop(0, last, sel_step, (init, init))
    scores(0, last)
    o_w.append(window_out(1))
    carry = (carry[0], accumulate(1, last, carry[1], True))
    carry = (accumulate(0, last, carry[0], True), carry[1])

    gates = jax.nn.sigmoid(gt_ref[...] + gb_ref[...])
    for g in groups:
        acc_s = carry[g][1]
        o_s = acc_s[:dh] / jnp.maximum(acc_s[dh:dh + 1], 1e-30)
        gates_g = gates[g * NSA_GATE_ROWS:(g + 1) * NSA_GATE_ROWS]
        for h in range(hpg):
            r = slice(g * qrows + h * dh, g * qrows + (h + 1) * dh)
            o_ref[r, :] = (oc_ref[r, :] + _gate_row(gates_g, h, 1) * o_s[:, hs(h)]
                           + _gate_row(gates_g, h, 2) * o_w[g][:, hs(h)])


def _nsa_attn(qt, ngt, gate_b, kcmp, vcmpt, ks, vst, kw, vwt):
    bsz, g_n, seq, dh = kw.shape
    nr = kcmp.shape[2]
    n_sel = seq // NSA_SEL_BLOCK
    n_qb = seq // NSA_QBLOCK
    n_cmp = (seq - NSA_CMP_BLOCK) // NSA_CMP_STRIDE + 1
    cmp_start = jnp.arange(nr) * NSA_CMP_STRIDE
    sel_start = jnp.arange(n_sel) * NSA_SEL_BLOCK
    ovl = ((cmp_start[None, :] < sel_start[:, None] + NSA_SEL_BLOCK)
           & (cmp_start[None, :] + NSA_CMP_BLOCK > sel_start[:, None])
           & (jnp.arange(nr)[None, :] < n_cmp)).astype(BF16)
    gb = jnp.pad(gate_b.reshape(g_n, NSA_HPG * NSA_BRANCHES), ((0, 0), (0, NSA_GATE_ROWS - NSA_HPG * NSA_BRANCHES)))
    gb = gb.reshape(g_n, NSA_GATE_ROWS, 1)
    qrows = NSA_HPG * dh
    full = lambda a, c: pl.BlockSpec((None, None, a, c), lambda b, g, n: (b, g, 0, 0))
    cols = lambda r, w: pl.BlockSpec((None, r, w), lambda b, g, n: (b, g, n))
    gb_spec = pl.BlockSpec((None, NSA_GATE_ROWS, 1), lambda b, g, n: (g, 0, 0))
    sw = min(SEL_QBLOCKS * NSA_QBLOCK, seq)
    oc, neg = pl.pallas_call(
        _nsa_select_kernel,
        grid=(bsz, g_n, seq // sw),
        in_specs=[cols(qrows, sw), cols(NSA_GATE_ROWS, sw), gb_spec,
                  full(nr, dh), full(dh, nr), _const_spec((n_sel, nr))],
        out_specs=(cols(qrows, sw), pl.BlockSpec((None, None, n_sel, sw), lambda b, g, n: (b, g, 0, n))),
        out_shape=(jax.ShapeDtypeStruct((bsz, g_n * qrows, seq), F32),
                   jax.ShapeDtypeStruct((bsz, g_n, n_sel, seq), F32)),
        compiler_params=_cparams("parallel", "parallel", "parallel"),
        name="nsa_select",
    )(qt, ngt, gb, kcmp, vcmpt, ovl)
    qb = NSA_QBLOCK
    qcols = lambda r: pl.BlockSpec((None, r, qb), lambda b, n: (b, 0, n))
    whole = lambda a, c: pl.BlockSpec((None, g_n, a, c), lambda b, n: (b, 0, 0, 0))
    return pl.pallas_call(
        _nsa_attn_kernel,
        grid=(bsz, n_qb),
        in_specs=[qcols(g_n * qrows), qcols(g_n * NSA_GATE_ROWS), _const_spec((g_n * NSA_GATE_ROWS, 1)),
                  qcols(g_n * qrows), pl.BlockSpec((None, g_n, n_sel, qb), lambda b, n: (b, 0, 0, n)),
                  whole(seq, LANE), whole(NSA_VROWS, seq), whole(seq, dh), whole(NSA_VROWS, seq)],
        out_specs=qcols(g_n * qrows),
        out_shape=jax.ShapeDtypeStruct((bsz, g_n * qrows, seq), F32),
        scratch_shapes=[pltpu.VMEM((g_n, LANE, NSA_HPG * qb), BF16),
                        pltpu.VMEM((g_n, SEL_KV_TILE, NSA_HPG * qb), F32),
                        pltpu.VMEM((g_n, min(NSA_WINDOW + qb, seq), NSA_HPG * qb), F32)],
        compiler_params=_cparams("parallel", "parallel"),
        name="nsa_attn",
    )(qt, ngt, gb.reshape(g_n * NSA_GATE_ROWS, 1), oc, neg, ks, vst, kw, vwt)


def _ffn_residual(h, g_in_ref, w1_ref, w2_ref, g_out_ref):
    xn = _rms(h, g_in_ref[...]).astype(BF16)
    hid = w1_ref.shape[1]
    step = 1024
    acc = jnp.zeros(h.shape, F32)
    for j in range(0, hid, step):
        a = jnp.maximum(_dot(xn, w1_ref[:, j:j + step]), 0.0)
        acc = acc + _dot((a * a).astype(BF16), w2_ref[j:j + step, :])
    return h + _rms(acc, g_out_ref[...])


def _even_out_ffn_kernel(h_ref, oa_ref, obt_ref, w_ref, g_ref, g_in_ref, w1_ref, w2_ref, g_out_ref, o_ref):
    m = _dot(oa_ref[...].astype(BF16), w_ref[:GLA_V, :]) + lax.dot_general(
        obt_ref[...].astype(BF16), w_ref[GLA_V:, :], (((0,), (0,)), ((), ())), preferred_element_type=F32)
    h = h_ref[...] + _rms(m, g_ref[...])
    o_ref[...] = _ffn_residual(h, g_in_ref, w1_ref, w2_ref, g_out_ref)


def _even_out_ffn(h, o_a, o_bt, w_out, g, g_in, w1, w2, g_out):
    bsz, seq, d = h.shape
    tm = min(ROW_TILE, seq)
    row = lambda w: pl.BlockSpec((None, tm, w), lambda b, i: (b, i, 0))
    return pl.pallas_call(
        _even_out_ffn_kernel,
        grid=(bsz, seq // tm),
        in_specs=[row(d), row(GLA_V), pl.BlockSpec((None, NSA_Q, tm), lambda b, i: (b, 0, i)),
                  _const_spec(w_out.shape), _const_spec((1, d)),
                  _const_spec((1, d)), _const_spec(w1.shape), _const_spec(w2.shape), _const_spec((1, d))],
        out_specs=row(d),
        out_shape=jax.ShapeDtypeStruct(h.shape, F32),
        compiler_params=_cparams("parallel", "parallel"),
        name="even_out_ffn",
    )(h, o_a, o_bt, w_out.astype(BF16), g.reshape(1, d),
      g_in.reshape(1, d), w1.astype(BF16), w2.astype(BF16), g_out.reshape(1, d))


def _ffn_kernel(h_ref, g_in_ref, w1_ref, w2_ref, g_out_ref, o_ref):
    o_ref[...] = _ffn_residual(h_ref[...], g_in_ref, w1_ref, w2_ref, g_out_ref)


def _ffn(h, g_in, w1, w2, g_out):
    bsz, seq, d = h.shape
    tm = min(ROW_TILE, seq)
    row = pl.BlockSpec((None, tm, d), lambda b, i: (b, i, 0))
    return pl.pallas_call(
        _ffn_kernel,
        grid=(bsz, seq // tm),
        in_specs=[row, _const_spec((1, d)), _const_spec(w1.shape), _const_spec(w2.shape), _const_spec((1, d))],
        out_specs=row,
        out_shape=jax.ShapeDtypeStruct(h.shape, F32),
        compiler_params=_cparams("parallel", "parallel"),
        name="ffn",
    )(h, g_in.reshape(1, d), w1.astype(BF16), w2.astype(BF16), g_out.reshape(1, d))


def _sgu_kernel(h_ref, g_in_ref, w_in_ref, ln_g_ref, ln_b_ref, ws_ref, bs_ref, w_out_ref, g_out_ref, o_ref):
    e, c, gw = SGU_WIDTH, SGU_CHUNK, SGU_WIDTH // SGU_GROUPS
    h = h_ref[...]
    tm = h.shape[0]
    xn = _rms(h, g_in_ref[...]).astype(BF16)
    u = jax.nn.gelu(_dot(xn, w_in_ref[:, :e]))
    v = jax.nn.gelu(_dot(xn, w_in_ref[:, e:]))
    mu = jnp.mean(v, axis=-1, keepdims=True)
    var = jnp.mean((v - mu) ** 2, axis=-1, keepdims=True)
    vn = ((v - mu) * lax.rsqrt(var + NORM_EPS) * ln_g_ref[...] + ln_b_ref[...]).astype(BF16)
    causal = lax.broadcasted_iota(jnp.int32, (c, c), 0) >= lax.broadcasted_iota(jnp.int32, (c, c), 1)
    w_c = [jnp.where(causal, ws_ref[g], 0.0).astype(BF16) for g in range(SGU_GROUPS)]
    rows = []
    for r0 in range(0, tm, c):
        mixed = [_dot(w_c[g], vn[r0:r0 + c, g * gw:(g + 1) * gw]) for g in range(SGU_GROUPS)]
        rows.append(jnp.concatenate(mixed, axis=1) + bs_ref[...])
    y = (u * jnp.concatenate(rows, axis=0)).astype(BF16)
    o_ref[...] = h + _rms(_dot(y, w_out_ref[...]), g_out_ref[...])


def _sgu(h, g_in, w_in, ln_g, ln_b, w_s, b_s, w_out, g_out):
    bsz, seq, d = h.shape
    tm = min(SGU_ROW_TILE, seq)
    e = SGU_WIDTH
    bias = jnp.repeat(b_s.T, e // SGU_GROUPS, axis=1)
    row = pl.BlockSpec((None, tm, d), lambda b, i: (b, i, 0))
    return pl.pallas_call(
        _sgu_kernel,
        grid=(bsz, seq // tm),
        in_specs=[row, _const_spec((1, d)), _const_spec(w_in.shape), _const_spec((1, e)), _const_spec((1, e)),
                  _const_spec(w_s.shape), _const_spec(bias.shape), _const_spec(w_out.shape), _const_spec((1, d))],
        out_specs=row,
        out_shape=jax.ShapeDtypeStruct(h.shape, F32),
        compiler_params=_cparams("parallel", "parallel"),
        name="sgu",
    )(h, g_in.reshape(1, d), w_in.astype(BF16), ln_g.reshape(1, e), ln_b.reshape(1, e), w_s, bias,
      w_out.astype(BF16), g_out.reshape(1, d))


def _even_mixer_outputs(h, g_in, w_in, w_gate, b_gate, gla_norm, gate_b, cmp_pos, cmp_w1, cmp_w2):
    (gq, gk, gv, gvt, gr, glog, kc, vc, ks, kw, qt, vst, vwt, ngt) = _even_proj(h, g_in, w_in, w_gate, b_gate)
    o_a = _gla(gq, gk, gv, gvt, gr, glog, gla_norm)
    kcmp, vcmpt = _nsa_compress(kc, vc, cmp_pos, cmp_w1, cmp_w2)
    return o_a, _nsa_attn(qt, ngt, gate_b, kcmp, vcmpt, ks, vst, kw, vwt)


def kernel(x, norm_g, ffn_w1, ffn_w2, e_w_in, e_w_out, gla_w_gate, gla_b_gate, gla_norm, nsa_gate_b, nsa_cmp_pos, nsa_cmp_w1, nsa_cmp_w2, o_w_in, o_ln_g, o_ln_b, o_w_s, o_b_s, o_w_out):
    h = x
    depth = norm_g.shape[0]
    for layer in range(depth):
        i = layer // 2
        ffn_args = (norm_g[layer, 2], ffn_w1[layer], ffn_w2[layer], norm_g[layer, 3])
        if layer % 2 == 0:
            o_a, o_bt = _even_mixer_outputs(h, norm_g[layer, 0], e_w_in[i], gla_w_gate[i], gla_b_gate[i],
                                            gla_norm[i], nsa_gate_b[i], nsa_cmp_pos[i], nsa_cmp_w1[i], nsa_cmp_w2[i])
            h = _even_out_ffn(h, o_a, o_bt, e_w_out[i], norm_g[layer, 1], *ffn_args)
        else:
            h = _sgu(h, norm_g[layer, 0], o_w_in[i], o_ln_g[i], o_ln_b[i], o_w_s[i], o_b_s[i], o_w_out[i],
                     norm_g[layer, 1])
            h = _ffn(h, *ffn_args)
    return h
```

```python
import functools

import jax
import jax.numpy as jnp
from jax import lax
from jax.experimental import pallas as pl
from jax.experimental.pallas import tpu as pltpu

F32 = jnp.float32
BF16 = jnp.bfloat16

D_MODEL = 1024
GLA_HEADS = 4
GLA_DK = 64
GLA_DV = 128
GLA_GATE_RANK = 16
GLA_GATE_TEMP = 16.0
GLA_CHUNK = 64
NSA_HEADS = 8
NSA_KV_GROUPS = 2
NSA_HPG = NSA_HEADS // NSA_KV_GROUPS
NSA_DH = 64
NSA_CMP_BLOCK = 32
NSA_CMP_STRIDE = 16
NSA_CMP_HIDDEN = 128
NSA_SEL_BLOCK = 64
NSA_TOPN = 16
NSA_WINDOW = 512
NSA_QBLOCK = 128
NSA_BRANCHES = 3
SGU_WIDTH = 2 * D_MODEL
SGU_GROUPS = 8
SGU_CHUNK = 128
FFN_HIDDEN = 4 * D_MODEL
NORM_EPS = 1e-6
NEG_BIG = -1e30
POS_BIG = 1e30

GLA_QK = GLA_HEADS * GLA_DK
GLA_V = GLA_HEADS * GLA_DV
NSA_Q = NSA_HEADS * NSA_DH
NSA_KV = NSA_KV_GROUPS * NSA_DH
NSA_GATE_ROWS = 16
NSA_VROWS = NSA_DH + 16
LOG2E = 1.4426950408889634
LANE = 128

VMEM_LIMIT_BYTES = 56 * 1024 * 1024

ROW_TILE = 512
SGU_ROW_TILE = 512
GLA_ROW_TILE = 256
SEL_KV_TILE = 512
ATTN_QBLOCK = 256
SEL_VARIANTS = 4
SEL_QBLOCKS = 4


def _cparams(*sem):
    return pltpu.CompilerParams(dimension_semantics=sem, vmem_limit_bytes=VMEM_LIMIT_BYTES)


def _rms(x, g):
    return x * lax.rsqrt(jnp.mean(x * x, axis=-1, keepdims=True) + NORM_EPS) * g


def _dot(a, b):
    return jnp.dot(a, b, preferred_element_type=F32)


def _dot_nt(a, b):
    return lax.dot_general(a, b, (((1,), (1,)), ((), ())), preferred_element_type=F32)


def _split3(x):
    x1 = x.astype(BF16)
    r1 = x - x1.astype(F32)
    x2 = r1.astype(BF16)
    r2 = r1 - x2.astype(F32)
    return x1, x2, r2.astype(BF16)


def _dot_exact_lhs(a_bf16, b_f32):
    b1, b2, b3 = _split3(b_f32)
    return _dot(a_bf16, b1) + _dot(a_bf16, b2) + _dot(a_bf16, b3)


def _const_spec(shape):
    return pl.BlockSpec(shape, lambda *_: (0,) * len(shape))


def _even_proj_kernel(h_ref, g_ref, wrow_ref, wt_ref, wlr_ref, wgate_ref, bgate_ref,
                      gq_ref, gk_ref, gv_ref, gvt_ref, gr_ref, glog_ref,
                      kc_ref, vc_ref, ks_ref, kw_ref, qt_ref, vst_ref, vwt_ref, ngt_ref):
    xn = _rms(h_ref[...], g_ref[...]).astype(BF16)
    o = 0
    gq_ref[...] = _dot(xn, wrow_ref[:, o:o + GLA_QK]) * (GLA_DK ** -0.5)
    o += GLA_QK
    gk_ref[...] = _dot(xn, wrow_ref[:, o:o + GLA_QK])
    o += GLA_QK
    gv_ref[...] = _dot(xn, wrow_ref[:, o:o + GLA_V]).astype(BF16)
    o += GLA_V
    gr_ref[...] = _dot(xn, wrow_ref[:, o:o + GLA_V])
    o += GLA_V
    tm = xn.shape[0]
    for ref in (kc_ref, vc_ref, kw_ref):
        y = _dot(xn, wrow_ref[:, o:o + NSA_KV]).astype(BF16)
        for g in range(NSA_KV_GROUPS):
            ref[g] = y[:, g * NSA_DH:(g + 1) * NSA_DH]
        o += NSA_KV
    y = _dot(xn, wrow_ref[:, o:o + NSA_KV])
    lane = lax.broadcasted_iota(jnp.int32, (tm, NSA_DH), 1)
    blk = lax.broadcasted_iota(jnp.int32, (tm, NSA_DH), 0) // NSA_SEL_BLOCK
    onehot = jnp.where(lane == blk % (SEL_KV_TILE // NSA_SEL_BLOCK), 1.0, 0.0)
    for g in range(NSA_KV_GROUPS):
        ks_ref[g] = jnp.concatenate([y[:, g * NSA_DH:(g + 1) * NSA_DH], onehot], axis=1).astype(BF16)
    o = 0
    gvt_ref[...] = _dot_nt(wt_ref[o:o + GLA_V, :], xn).astype(BF16)
    o += GLA_V
    qt_ref[...] = (_dot_nt(wt_ref[o:o + NSA_Q, :], xn) * (NSA_DH ** -0.5 * LOG2E)).astype(BF16)
    o += NSA_Q
    ones_row = jnp.where(lax.broadcasted_iota(jnp.int32, (NSA_VROWS - NSA_DH, tm), 0) == 0, 1.0, 0.0)
    for ref in (vst_ref, vwt_ref):
        y = _dot_nt(wt_ref[o:o + NSA_KV, :], xn)
        for g in range(NSA_KV_GROUPS):
            ref[g] = jnp.concatenate([y[g * NSA_DH:(g + 1) * NSA_DH], ones_row], axis=0).astype(BF16)
        o += NSA_KV
    ngt_ref[...] = _dot_nt(wt_ref[o:o + 2 * NSA_GATE_ROWS, :], xn)
    lr = _dot(xn, wlr_ref[...])
    l1, l2, l3 = _split3(lr)
    w1, w2, w3 = _split3(wgate_ref[...])
    z = (_dot(l1, w1) + _dot(l1, w2) + _dot(l2, w1) + _dot(l2, w2) + _dot(l1, w3) + _dot(l3, w1)
         + bgate_ref[...])
    glog_ref[...] = jax.nn.log_sigmoid(z) * (1.0 / GLA_GATE_TEMP)


def _even_proj(h, g, w_in, w_gate, b_gate):
    bsz, seq, d = h.shape
    tm = min(ROW_TILE, seq)
    sizes = (GLA_QK, GLA_QK, GLA_V, GLA_GATE_RANK, GLA_V, NSA_Q) + (NSA_KV,) * 6 + (NSA_HEADS * NSA_BRANCHES,)
    offs = [0]
    for s in sizes:
        offs.append(offs[-1] + s)
    col = lambda i: w_in[:, offs[i]:offs[i + 1]]
    gq, gk, gv, glr, gr, nq, kc, vc, ks, vs, kw, vw, ng = [col(i) for i in range(13)]
    wrow = jnp.concatenate([gq, gk, gv, gr, kc, vc, kw, ks], axis=1).astype(BF16)
    ng_g = ng.reshape(d, NSA_KV_GROUPS, NSA_HPG * NSA_BRANCHES)
    ng_g = jnp.pad(ng_g, ((0, 0), (0, 0), (0, NSA_GATE_ROWS - NSA_HPG * NSA_BRANCHES)))
    wt = jnp.concatenate([gv, nq, vs, vw, ng_g.reshape(d, NSA_KV_GROUPS * NSA_GATE_ROWS)], axis=1).T.astype(BF16)
    wlr = jnp.pad(glr, ((0, 0), (0, LANE - GLA_GATE_RANK))).astype(BF16)
    wgate = jnp.pad(w_gate, ((0, LANE - GLA_GATE_RANK), (0, 0)))
    n_t = wt.shape[0]

    row = lambda w: pl.BlockSpec((None, tm, w), lambda b, i: (b, i, 0))
    grp = pl.BlockSpec((None, NSA_KV_GROUPS, tm, NSA_DH), lambda b, i: (b, 0, i, 0))
    tr = lambda w: pl.BlockSpec((None, w, tm), lambda b, i: (b, 0, i))
    grp_ext = pl.BlockSpec((None, NSA_KV_GROUPS, tm, LANE), lambda b, i: (b, 0, i, 0))
    grp_tr = pl.BlockSpec((None, NSA_KV_GROUPS, NSA_VROWS, tm), lambda b, i: (b, 0, 0, i))
    sds = jax.ShapeDtypeStruct
    kv_shape = sds((bsz, NSA_KV_GROUPS, seq, NSA_DH), BF16)
    vt_shape = sds((bsz, NSA_KV_GROUPS, NSA_VROWS, seq), BF16)
    out_shape = (
        sds((bsz, seq, GLA_QK), F32), sds((bsz, seq, GLA_QK), F32), sds((bsz, seq, GLA_V), BF16),
        sds((bsz, GLA_V, seq), BF16), sds((bsz, seq, GLA_V), F32), sds((bsz, seq, GLA_QK), F32),
        kv_shape, kv_shape, sds((bsz, NSA_KV_GROUPS, seq, LANE), BF16), kv_shape,
        sds((bsz, NSA_Q, seq), BF16), vt_shape, vt_shape,
        sds((bsz, NSA_KV_GROUPS * NSA_GATE_ROWS, seq), F32),
    )
    out_specs = (
        row(GLA_QK), row(GLA_QK), row(GLA_V), tr(GLA_V), row(GLA_V), row(GLA_QK),
        grp, grp, grp_ext, grp,
        tr(NSA_Q), grp_tr, grp_tr, tr(NSA_KV_GROUPS * NSA_GATE_ROWS),
    )
    return pl.pallas_call(
        _even_proj_kernel,
        grid=(bsz, seq // tm),
        in_specs=[
            row(d), _const_spec((1, d)), _const_spec(wrow.shape), _const_spec((n_t, d)),
            _const_spec((d, LANE)), _const_spec((LANE, GLA_QK)), _const_spec((1, GLA_QK)),
        ],
        out_specs=out_specs,
        out_shape=out_shape,
        compiler_params=_cparams("parallel", "parallel"),
        name="even_proj",
    )(h, g.reshape(1, d), wrow, wt, wlr, wgate, b_gate.reshape(1, GLA_QK))


def _gla_kernel(q_ref, k_ref, v_ref, vt_ref, r_ref, glog_ref, ng_ref, o_ref, state_ref):
    c = GLA_CHUNK
    ts = q_ref.shape[0]

    @pl.when(pl.program_id(1) == 0)
    def _():
        state_ref[...] = jnp.zeros_like(state_ref)

    ri = lax.broadcasted_iota(jnp.int32, (c, c), 0)
    ci = lax.broadcasted_iota(jnp.int32, (c, c), 1)
    causal = ri >= ci
    tril = jnp.where(causal, 1.0, 0.0).astype(BF16)
    row2 = lax.broadcasted_iota(jnp.int32, (2 * c, 1), 0)
    state = [state_ref[h] for h in range(GLA_HEADS)]

    for pair in range(ts // (2 * c)):
        p0 = pair * 2 * c
        vt_pair = vt_ref[:, p0:p0 + 2 * c]
        for half in range(2):
            r0 = p0 + half * c
            gl = glog_ref[r0:r0 + c, :]
            b = _dot_exact_lhs(tril, gl)
            b_last = b[c - 1:c, :]
            q_t = (q_ref[r0:r0 + c, :] * jnp.exp(b)).astype(BF16)
            k = k_ref[r0:r0 + c, :]
            k_t = (k * jnp.exp(-b)).astype(BF16)
            k_end = (k * jnp.exp(b_last - b)).astype(BF16)
            decay = jnp.exp(b_last)
            in_half = (row2 >= half * c) & (row2 < (half + 1) * c)
            for h in range(GLA_HEADS):
                ks = slice(h * GLA_DK, (h + 1) * GLA_DK)
                vs = slice(h * GLA_DV, (h + 1) * GLA_DV)
                a = jnp.where(causal, _dot_nt(q_t[:, ks], k_t[:, ks]), 0.0).astype(BF16)
                o = _dot(a, v_ref[r0:r0 + c, vs]) + _dot_nt(q_t[:, ks], state[h].astype(BF16))
                ke_pair = jnp.where(in_half, jnp.concatenate([k_end[:, ks], k_end[:, ks]], axis=0), 0.0)
                state[h] = state[h] * decay[:, ks] + _dot(vt_pair[vs, :], ke_pair.astype(BF16))
                o = o * lax.rsqrt(jnp.mean(o * o, axis=-1, keepdims=True) + NORM_EPS) * ng_ref[:, vs]
                o_ref[r0:r0 + c, vs] = o * jax.nn.silu(r_ref[r0:r0 + c, vs])
    for h in range(GLA_HEADS):
        state_ref[h] = state[h]


def _gla(gq, gk, gv, gvt, gr, glog, gla_norm):
    bsz, seq, _ = gq.shape
    ts = min(GLA_ROW_TILE, seq)
    row = lambda w: pl.BlockSpec((None, ts, w), lambda b, i: (b, i, 0))
    return pl.pallas_call(
        _gla_kernel,
        grid=(bsz, seq // ts),
        in_specs=[row(GLA_QK), row(GLA_QK), row(GLA_V),
                  pl.BlockSpec((None, GLA_V, ts), lambda b, i: (b, 0, i)),
                  row(GLA_V), row(GLA_QK), _const_spec((1, GLA_V))],
        out_specs=row(GLA_V),
        out_shape=jax.ShapeDtypeStruct((bsz, seq, GLA_V), F32),
        scratch_shapes=[pltpu.VMEM((GLA_HEADS, GLA_DV, GLA_DK), F32)],
        compiler_params=_cparams("parallel", "arbitrary"),
        name="gla",
    )(gq, gk, gv, gvt, gr, glog, gla_norm.reshape(1, GLA_V))


def _nsa_compress_kernel(kc_ref, vc_ref, pos_ref, w1_ref, w2_ref, w2t_ref, kcmp_ref, vcmpt_ref):
    half = (NSA_CMP_BLOCK // 2) * NSA_DH

    def hidden(x_ref, i):
        x = x_ref[...]
        first = _dot(x, w1_ref[i, :half, :])
        second = _dot(x, w1_ref[i, half:, :])
        nr = first.shape[0]
        nxt = jnp.concatenate([second[1:], second[:1]], axis=0)
        posc = _dot(pos_ref[i], w1_ref[i])[0:1]
        return jax.nn.gelu(first + nxt + posc).astype(BF16)

    kcmp_ref[...] = _dot(hidden(kc_ref, 0), w2_ref[0]).astype(BF16)
    vcmpt_ref[...] = _dot_nt(w2t_ref[...], hidden(vc_ref, 1)).astype(BF16)


def _nsa_compress(kc, vc, cmp_pos, cmp_w1, cmp_w2):
    bsz, g_n, seq, dh = kc.shape
    grp_tok = NSA_CMP_BLOCK // 2
    nr = seq // grp_tok
    feat = grp_tok * dh
    kc_r = kc.reshape(bsz, g_n, nr, feat)
    vc_r = vc.reshape(bsz, g_n, nr, feat)
    pos = jnp.broadcast_to(cmp_pos.reshape(2, 1, NSA_CMP_BLOCK * dh), (2, 8, NSA_CMP_BLOCK * dh)).astype(BF16)
    blk = pl.BlockSpec((None, None, nr, feat), lambda b, g: (b, g, 0, 0))
    return pl.pallas_call(
        _nsa_compress_kernel,
        grid=(bsz, g_n),
        in_specs=[blk, blk, _const_spec(pos.shape), _const_spec(cmp_w1.shape), _const_spec(cmp_w2.shape),
                  _const_spec((dh, NSA_CMP_HIDDEN))],
        out_specs=(pl.BlockSpec((None, None, nr, dh), lambda b, g: (b, g, 0, 0)),
                   pl.BlockSpec((None, None, dh, nr), lambda b, g: (b, g, 0, 0))),
        out_shape=(jax.ShapeDtypeStruct((bsz, g_n, nr, dh), BF16),
                   jax.ShapeDtypeStruct((bsz, g_n, dh, nr), BF16)),
        compiler_params=_cparams("parallel", "parallel"),
        name="nsa_compress",
    )(kc_r, vc_r, pos, cmp_w1.astype(BF16), cmp_w2.astype(BF16), cmp_w2[1].T.astype(BF16))


def _heads_on_lanes(qt_ref, row0=0):
    return jnp.concatenate([qt_ref[row0 + h * NSA_DH:row0 + (h + 1) * NSA_DH, :] for h in range(NSA_HPG)], axis=1)


def _gate_row(gates, h, branch):
    r = h * NSA_BRANCHES + branch
    return gates[r:r + 1, :]


def _nsa_select_kernel(qt_ref, gt_ref, gb_ref, kcmp_ref, vcmpt_ref, ovl_ref, oc_ref, neg_ref):
    dh, hpg = NSA_DH, NSA_HPG
    w = qt_ref.shape[1]
    nr = kcmp_ref.shape[0]
    n_sel = neg_ref.shape[0]
    step = pl.program_id(2)
    hs = lambda h: slice(h * w, (h + 1) * w)

    def select(ns):
        cr = ns * (nr // n_sel)
        q = _heads_on_lanes(qt_ref)
        t = step * w + lax.broadcasted_iota(jnp.int32, (1, w), 1)

        s_c = _dot(kcmp_ref[:cr, :], q)
        cmp_end = lax.broadcasted_iota(jnp.int32, (cr, 1), 0) * NSA_CMP_STRIDE + (NSA_CMP_BLOCK - 1)
        ok_c = cmp_end <= t
        has_key = t >= NSA_CMP_BLOCK - 1
        p_c = []
        for h in range(hpg):
            s_h = jnp.where(ok_c, s_c[:, hs(h)], NEG_BIG)
            e = jnp.exp2(s_h - jnp.max(s_h, axis=0, keepdims=True))
            p_c.append(e * jnp.where(has_key, 1.0 / jnp.sum(e, axis=0, keepdims=True), 0.0))
        o_c = _dot(vcmpt_ref[:, :cr], jnp.concatenate(p_c, axis=1).astype(BF16))

        p1, p2, _ = _split3(sum(p_c[1:], p_c[0]))
        imp = _dot(ovl_ref[:ns, :cr], p1) + _dot(ovl_ref[:ns, :cr], p2)
        j = lax.broadcasted_iota(jnp.int32, (ns, 1), 0)
        jf = j.astype(F32)
        cur = t // NSA_SEL_BLOCK
        forced = (j == 0) | (j == cur) | (j == cur - 1)
        score = jnp.where(forced, POS_BIG, jnp.where(j <= cur, imp, NEG_BIG))
        for _ in range(min(NSA_TOPN, ns)):
            top = jnp.max(score, axis=0, keepdims=True)
            first = jnp.min(jnp.where(score == top, jf, float(ns)), axis=0, keepdims=True)
            score = jnp.where(jf == first, -jnp.inf, score)
        neg_ref[:ns, :] = jnp.where((score == -jnp.inf) & (j <= cur), 0.0, NEG_BIG)
        if ns < n_sel:
            neg_ref[ns:, :] = jnp.full((n_sel - ns, w), NEG_BIG, F32)

        gates = jax.nn.sigmoid(gt_ref[...] + gb_ref[...])
        for h in range(hpg):
            oc_ref[h * dh:(h + 1) * dh, :] = _gate_row(gates, h, 0) * o_c[:, hs(h)]

    n_var = max(1, min(SEL_VARIANTS, n_sel // NSA_TOPN))
    blocks_per_step = w // NSA_SEL_BLOCK
    lo = 0
    for v in range(1, n_var + 1):
        ns = n_sel * v // n_var
        hi = ns // blocks_per_step
        pl.when((step >= lo) & (step < hi))(functools.partial(select, ns))
        lo = hi


def _nsa_attn_kernel(qt_ref, gt_ref, gb_ref, oc_ref, neg_ref, ks_ref, vst_ref, kw_ref, vwt_ref, o_ref,
                     qext_ref, s_ref, smax_ref, sw_ref):
    qb, dh, hpg, sb = qt_ref.shape[1], NSA_DH, NSA_HPG, NSA_SEL_BLOCK
    assert NSA_KV_GROUPS == 2
    groups = range(NSA_KV_GROUPS)
    qrows = hpg * dh
    n = pl.program_id(1)
    t0 = n * qb
    seq = kw_ref.shape[1]
    hs = lambda h: slice(h * qb, (h + 1) * qb)
    q = [_heads_on_lanes(qt_ref, g * qrows) for g in groups]
    t = t0 + lax.broadcasted_iota(jnp.int32, (1, qb), 1)

    tk = SEL_KV_TILE
    bpt = tk // sb
    mrows = 16
    for g in groups:
        qext_ref[g, 0:dh, :] = q[g]
        qext_ref[g, dh + mrows:, :] = jnp.zeros((qext_ref.shape[1] - dh - mrows, hpg * qb), BF16)
    kpos_tile = lax.broadcasted_iota(jnp.int32, (tk, 1), 0)

    def scores(g, kt):
        rows = neg_ref[g, pl.ds(pl.multiple_of(kt * bpt, bpt), bpt), :]
        ext = jnp.concatenate([rows, jnp.zeros((mrows - bpt, qb), F32)], axis=0).astype(BF16)
        qext_ref[g, dh:dh + mrows, :] = jnp.concatenate([ext] * hpg, axis=1)
        s = _dot(ks_ref[g, pl.ds(pl.multiple_of(kt * tk, tk), tk), :], qext_ref[g])
        s_ref[g] = s
        smax_ref[g] = jnp.max(s, axis=0, keepdims=True)

    def accumulate(g, kt, carry_g, diagonal):
        m, acc = carry_g
        k0 = pl.multiple_of(kt * tk, tk)
        if diagonal:
            ok = kpos_tile + k0 <= t
            s_d = [jnp.where(ok, s_ref[g, :, hs(h)], NEG_BIG) for h in range(hpg)]
            m_new = jnp.maximum(m, jnp.concatenate([jnp.max(s_h, axis=0, keepdims=True) for s_h in s_d], axis=1))
        else:
            m_new = jnp.maximum(m, smax_ref[g])
        p = []
        for h in range(hpg):
            s_h = s_d[h] if diagonal else s_ref[g, :, hs(h)]
            p.append(jnp.exp2(s_h - m_new[:, hs(h)]).astype(BF16))
        acc = jnp.exp2(m - m_new) * acc + _dot(vst_ref[g, :, pl.ds(k0, tk)], jnp.concatenate(p, axis=1))
        return m_new, acc

    def sel_step(kt, carry):
        scores(0, kt)
        c1 = accumulate(1, kt, carry[1], False)
        scores(1, kt + 1)
        c0 = accumulate(0, kt, carry[0], False)
        return c0, c1

    wk = sw_ref.shape[1]
    start = pl.multiple_of(jnp.maximum(t0 + qb - wk, 0), qb)
    kpos = start + lax.broadcasted_iota(jnp.int32, (wk, 1), 0)
    ok_w = (kpos <= t) & (kpos > t - NSA_WINDOW)

    def window_scores(g):
        sw_ref[g] = _dot(kw_ref[g, pl.ds(start, wk), :], q[g])

    def window_out(g):
        p_w = []
        for h in range(hpg):
            s_h = jnp.where(ok_w, sw_ref[g, :, hs(h)], NEG_BIG)
            p_w.append(jnp.exp2(s_h - jnp.max(s_h, axis=0, keepdims=True)).astype(BF16))
        acc_w = _dot(vwt_ref[g, :, pl.ds(start, wk)], jnp.concatenate(p_w, axis=1))
        return acc_w[:dh] / jnp.maximum(acc_w[dh:dh + 1], 1e-30)

    last = (n + tk // qb) // (tk // qb) - 1
    init = (jnp.full((1, hpg * qb), NEG_BIG, F32), jnp.zeros((NSA_VROWS, hpg * qb), F32))
    window_scores(0)
    scores(1, 0)
    o_w = [window_out(0)]
    window_scores(1)
    carry = lax.fori_loop(0, last, sel_step, (init, init))
    scores(0, last)
    o_w.append(window_out(1))
    carry = (carry[0], accumulate(1, last, carry[1], True))
    carry = (accumulate(0, last, carry[0], True), carry[1])

    gates = jax.nn.sigmoid(gt_ref[...] + gb_ref[...])
    for g in groups:
        acc_s = carry[g][1]
        o_s = acc_s[:dh] / jnp.maximum(acc_s[dh:dh + 1], 1e-30)
        gates_g = gates[g * NSA_GATE_ROWS:(g + 1) * NSA_GATE_ROWS]
        for h in range(hpg):
            r = slice(g * qrows + h * dh, g * qrows + (h + 1) * dh)
            o_ref[r, :] = (oc_ref[r, :] + _gate_row(gates_g, h, 1) * o_s[:, hs(h)]
                           + _gate_row(gates_g, h, 2) * o_w[g][:, hs(h)])


def _nsa_attn(qt, ngt, gate_b, kcmp, vcmpt, ks, vst, kw, vwt):
    bsz, g_n, seq, dh = kw.shape
    nr = kcmp.shape[2]
    n_sel = seq // NSA_SEL_BLOCK
    n_cmp = (seq - NSA_CMP_BLOCK) // NSA_CMP_STRIDE + 1
    cmp_start = jnp.arange(nr) * NSA_CMP_STRIDE
    sel_start = jnp.arange(n_sel) * NSA_SEL_BLOCK
    ovl = ((cmp_start[None, :] < sel_start[:, None] + NSA_SEL_BLOCK)
           & (cmp_start[None, :] + NSA_CMP_BLOCK > sel_start[:, None])
           & (jnp.arange(nr)[None, :] < n_cmp)).astype(BF16)
    gb = jnp.pad(gate_b.reshape(g_n, NSA_HPG * NSA_BRANCHES), ((0, 0), (0, NSA_GATE_ROWS - NSA_HPG * NSA_BRANCHES)))
    gb = gb.reshape(g_n, NSA_GATE_ROWS, 1)
    qrows = NSA_HPG * dh
    full = lambda a, c: pl.BlockSpec((None, None, a, c), lambda b, g, n: (b, g, 0, 0))
    cols = lambda r, w: pl.BlockSpec((None, r, w), lambda b, g, n: (b, g, n))
    gb_spec = pl.BlockSpec((None, NSA_GATE_ROWS, 1), lambda b, g, n: (g, 0, 0))
    sw = min(SEL_QBLOCKS * NSA_QBLOCK, seq)
    oc, neg = pl.pallas_call(
        _nsa_select_kernel,
        grid=(bsz, g_n, seq // sw),
        in_specs=[cols(qrows, sw), cols(NSA_GATE_ROWS, sw), gb_spec,
                  full(nr, dh), full(dh, nr), _const_spec((n_sel, nr))],
        out_specs=(cols(qrows, sw), pl.BlockSpec((None, None, n_sel, sw), lambda b, g, n: (b, g, 0, n))),
        out_shape=(jax.ShapeDtypeStruct((bsz, g_n * qrows, seq), F32),
                   jax.ShapeDtypeStruct((bsz, g_n, n_sel, seq), F32)),
        compiler_params=_cparams("parallel", "parallel", "parallel"),
        name="nsa_select",
    )(qt, ngt, gb, kcmp, vcmpt, ovl)
    qb = min(ATTN_QBLOCK, seq)
    n_qb = seq // qb
    qcols = lambda r: pl.BlockSpec((None, r, qb), lambda b, n: (b, 0, n))
    whole = lambda a, c: pl.BlockSpec((None, g_n, a, c), lambda b, n: (b, 0, 0, 0))
    return pl.pallas_call(
        _nsa_attn_kernel,
        grid=(bsz, n_qb),
        in_specs=[qcols(g_n * qrows), qcols(g_n * NSA_GATE_ROWS), _const_spec((g_n * NSA_GATE_ROWS, 1)),
                  qcols(g_n * qrows), pl.BlockSpec((None, g_n, n_sel, qb), lambda b, n: (b, 0, 0, n)),
                  whole(seq, LANE), whole(NSA_VROWS, seq), whole(seq, dh), whole(NSA_VROWS, seq)],
        out_specs=qcols(g_n * qrows),
        out_shape=jax.ShapeDtypeStruct((bsz, g_n * qrows, seq), F32),
        scratch_shapes=[pltpu.VMEM((g_n, LANE, NSA_HPG * qb), BF16),
                        pltpu.VMEM((g_n, SEL_KV_TILE, NSA_HPG * qb), F32),
                        pltpu.VMEM((g_n, 1, NSA_HPG * qb), F32),
                        pltpu.VMEM((g_n, min(NSA_WINDOW + qb, seq), NSA_HPG * qb), F32)],
        compiler_params=_cparams("parallel", "parallel"),
        name="nsa_attn",
    )(qt, ngt, gb.reshape(g_n * NSA_GATE_ROWS, 1), oc, neg, ks, vst, kw, vwt)


def _ffn_residual(h, g_in_ref, w1_ref, w2_ref, g_out_ref):
    xn = _rms(h, g_in_ref[...]).astype(BF16)
    hid = w1_ref.shape[1]
    step = 1024
    acc = jnp.zeros(h.shape, F32)
    for j in range(0, hid, step):
        a = jnp.maximum(_dot(xn, w1_ref[:, j:j + step]), 0.0)
        acc = acc + _dot((a * a).astype(BF16), w2_ref[j:j + step, :])
    return h + _rms(acc, g_out_ref[...])


def _even_out_ffn_kernel(h_ref, oa_ref, obt_ref, w_ref, g_ref, g_in_ref, w1_ref, w2_ref, g_out_ref, o_ref):
    m = _dot(oa_ref[...].astype(BF16), w_ref[:GLA_V, :]) + lax.dot_general(
        obt_ref[...].astype(BF16), w_ref[GLA_V:, :], (((0,), (0,)), ((), ())), preferred_element_type=F32)
    h = h_ref[...] + _rms(m, g_ref[...])
    o_ref[...] = _ffn_residual(h, g_in_ref, w1_ref, w2_ref, g_out_ref)


def _even_out_ffn(h, o_a, o_bt, w_out, g, g_in, w1, w2, g_out):
    bsz, seq, d = h.shape
    tm = min(ROW_TILE, seq)
    row = lambda w: pl.BlockSpec((None, tm, w), lambda b, i: (b, i, 0))
    return pl.pallas_call(
        _even_out_ffn_kernel,
        grid=(bsz, seq // tm),
        in_specs=[row(d), row(GLA_V), pl.BlockSpec((None, NSA_Q, tm), lambda b, i: (b, 0, i)),
                  _const_spec(w_out.shape), _const_spec((1, d)),
                  _const_spec((1, d)), _const_spec(w1.shape), _const_spec(w2.shape), _const_spec((1, d))],
        out_specs=row(d),
        out_shape=jax.ShapeDtypeStruct(h.shape, F32),
        compiler_params=_cparams("parallel", "parallel"),
        name="even_out_ffn",
    )(h, o_a, o_bt, w_out.astype(BF16), g.reshape(1, d),
      g_in.reshape(1, d), w1.astype(BF16), w2.astype(BF16), g_out.reshape(1, d))


def _ffn_kernel(h_ref, g_in_ref, w1_ref, w2_ref, g_out_ref, o_ref):
    o_ref[...] = _ffn_residual(h_ref[...], g_in_ref, w1_ref, w2_ref, g_out_ref)


def _ffn(h, g_in, w1, w2, g_out):
    bsz, seq, d = h.shape
    tm = min(ROW_TILE, seq)
    row = pl.BlockSpec((None, tm, d), lambda b, i: (b, i, 0))
    return pl.pallas_call(
        _ffn_kernel,
        grid=(bsz, seq // tm),
        in_specs=[row, _const_spec((1, d)), _const_spec(w1.shape), _const_spec(w2.shape), _const_spec((1, d))],
        out_specs=row,
        out_shape=jax.ShapeDtypeStruct(h.shape, F32),
        compiler_params=_cparams("parallel", "parallel"),
        name="ffn",
    )(h, g_in.reshape(1, d), w1.astype(BF16), w2.astype(BF16), g_out.reshape(1, d))


def _sgu_kernel(h_ref, g_in_ref, w_in_ref, ln_g_ref, ln_b_ref, ws_ref, bs_ref, w_out_ref, g_out_ref, o_ref):
    e, c, gw = SGU_WIDTH, SGU_CHUNK, SGU_WIDTH // SGU_GROUPS
    h = h_ref[...]
    tm = h.shape[0]
    xn = _rms(h, g_in_ref[...]).astype(BF16)
    u = jax.nn.gelu(_dot(xn, w_in_ref[:, :e]))
    v = jax.nn.gelu(_dot(xn, w_in_ref[:, e:]))
    mu = jnp.mean(v, axis=-1, keepdims=True)
    var = jnp.mean((v - mu) ** 2, axis=-1, keepdims=True)
    vn = ((v - mu) * lax.rsqrt(var + NORM_EPS) * ln_g_ref[...] + ln_b_ref[...]).astype(BF16)
    causal = lax.broadcasted_iota(jnp.int32, (c, c), 0) >= lax.broadcasted_iota(jnp.int32, (c, c), 1)
    w_c = [jnp.where(causal, ws_ref[g], 0.0).astype(BF16) for g in range(SGU_GROUPS)]
    rows = []
    for r0 in range(0, tm, c):
        mixed = [_dot(w_c[g], vn[r0:r0 + c, g * gw:(g + 1) * gw]) for g in range(SGU_GROUPS)]
        rows.append(jnp.concatenate(mixed, axis=1) + bs_ref[...])
    y = (u * jnp.concatenate(rows, axis=0)).astype(BF16)
    o_ref[...] = h + _rms(_dot(y, w_out_ref[...]), g_out_ref[...])


def _sgu(h, g_in, w_in, ln_g, ln_b, w_s, b_s, w_out, g_out):
    bsz, seq, d = h.shape
    tm = min(SGU_ROW_TILE, seq)
    e = SGU_WIDTH
    bias = jnp.repeat(b_s.T, e // SGU_GROUPS, axis=1)
    row = pl.BlockSpec((None, tm, d), lambda b, i: (b, i, 0))
    return pl.pallas_call(
        _sgu_kernel,
        grid=(bsz, seq // tm),
        in_specs=[row, _const_spec((1, d)), _const_spec(w_in.shape), _const_spec((1, e)), _const_spec((1, e)),
                  _const_spec(w_s.shape), _const_spec(bias.shape), _const_spec(w_out.shape), _const_spec((1, d))],
        out_specs=row,
        out_shape=jax.ShapeDtypeStruct(h.shape, F32),
        compiler_params=_cparams("parallel", "parallel"),
        name="sgu",
    )(h, g_in.reshape(1, d), w_in.astype(BF16), ln_g.reshape(1, e), ln_b.reshape(1, e), w_s, bias,
      w_out.astype(BF16), g_out.reshape(1, d))


def _even_mixer_outputs(h, g_in, w_in, w_gate, b_gate, gla_norm, gate_b, cmp_pos, cmp_w1, cmp_w2):
    (gq, gk, gv, gvt, gr, glog, kc, vc, ks, kw, qt, vst, vwt, ngt) = _even_proj(h, g_in, w_in, w_gate, b_gate)
    o_a = _gla(gq, gk, gv, gvt, gr, glog, gla_norm)
    kcmp, vcmpt = _nsa_compress(kc, vc, cmp_pos, cmp_w1, cmp_w2)
    return o_a, _nsa_attn(qt, ngt, gate_b, kcmp, vcmpt, ks, vst, kw, vwt)


def kernel(x, norm_g, ffn_w1, ffn_w2, e_w_in, e_w_out, gla_w_gate, gla_b_gate, gla_norm, nsa_gate_b, nsa_cmp_pos, nsa_cmp_w1, nsa_cmp_w2, o_w_in, o_ln_g, o_ln_b, o_w_s, o_b_s, o_w_out):
    h = x
    depth = norm_g.shape[0]
    for layer in range(depth):
        i = layer // 2
        ffn_args = (norm_g[layer, 2], ffn_w1[layer], ffn_w2[layer], norm_g[layer, 3])
        if layer % 2 == 0:
            o_a, o_bt = _even_mixer_outputs(h, norm_g[layer, 0], e_w_in[i], gla_w_gate[i], gla_b_gate[i],
                                            gla_norm[i], nsa_gate_b[i], nsa_cmp_pos[i], nsa_cmp_w1[i], nsa_cmp_w2[i])
            h = _even_out_ffn(h, o_a, o_bt, e_w_out[i], norm_g[layer, 1], *ffn_args)
        else:
            h = _sgu(h, norm_g[layer, 0], o_w_in[i], o_ln_g[i], o_ln_b[i], o_w_s[i], o_b_s[i], o_w_out[i],
                     norm_g[layer, 1])
            h = _ffn(h, *ffn_args)
    return h
```

```python
import functools

import jax
import jax.numpy as jnp
from jax import lax
from jax.experimental import pallas as pl
from jax.experimental.pallas import tpu as pltpu

F32 = jnp.float32
BF16 = jnp.bfloat16

D_MODEL = 1024
GLA_HEADS = 4
GLA_DK = 64
GLA_DV = 128
GLA_GATE_RANK = 16
GLA_GATE_TEMP = 16.0
GLA_CHUNK = 64
NSA_HEADS = 8
NSA_KV_GROUPS = 2
NSA_HPG = NSA_HEADS // NSA_KV_GROUPS
NSA_DH = 64
NSA_CMP_BLOCK = 32
NSA_CMP_STRIDE = 16
NSA_CMP_HIDDEN = 128
NSA_SEL_BLOCK = 64
NSA_TOPN = 16
NSA_WINDOW = 512
NSA_QBLOCK = 128
NSA_BRANCHES = 3
SGU_WIDTH = 2 * D_MODEL
SGU_GROUPS = 8
SGU_CHUNK = 128
FFN_HIDDEN = 4 * D_MODEL
NORM_EPS = 1e-6
NEG_BIG = -1e30
POS_BIG = 1e30

GLA_QK = GLA_HEADS * GLA_DK
GLA_V = GLA_HEADS * GLA_DV
NSA_Q = NSA_HEADS * NSA_DH
NSA_KV = NSA_KV_GROUPS * NSA_DH
NSA_GATE_ROWS = 16
NSA_VROWS = NSA_DH + 16
LOG2E = 1.4426950408889634
LANE = 128

VMEM_LIMIT_BYTES = 56 * 1024 * 1024

ROW_TILE = 512
SGU_ROW_TILE = 512
GLA_ROW_TILE = 256
SEL_KV_TILE = 512
ATTN_QBLOCK = 256
SEL_VARIANTS = 4
SEL_QBLOCKS = 4


def _cparams(*sem):
    return pltpu.CompilerParams(dimension_semantics=sem, vmem_limit_bytes=VMEM_LIMIT_BYTES)


def _rms(x, g):
    return x * lax.rsqrt(jnp.mean(x * x, axis=-1, keepdims=True) + NORM_EPS) * g


def _dot(a, b):
    return jnp.dot(a, b, preferred_element_type=F32)


def _dot_nt(a, b):
    return lax.dot_general(a, b, (((1,), (1,)), ((), ())), preferred_element_type=F32)


def _split3(x):
    x1 = x.astype(BF16)
    r1 = x - x1.astype(F32)
    x2 = r1.astype(BF16)
    r2 = r1 - x2.astype(F32)
    return x1, x2, r2.astype(BF16)


def _dot_exact_lhs(a_bf16, b_f32):
    b1, b2, b3 = _split3(b_f32)
    return _dot(a_bf16, b1) + _dot(a_bf16, b2) + _dot(a_bf16, b3)


def _const_spec(shape):
    return pl.BlockSpec(shape, lambda *_: (0,) * len(shape))


def _even_proj_kernel(h_ref, g_ref, wrow_ref, wt_ref, wlr_ref, wgate_ref, bgate_ref,
                      gq_ref, gk_ref, gv_ref, gvt_ref, gr_ref, glog_ref,
                      kc_ref, vc_ref, ks_ref, kw_ref, qt_ref, vst_ref, vwt_ref, ngt_ref):
    xn = _rms(h_ref[...], g_ref[...]).astype(BF16)
    o = 0
    gq_ref[...] = _dot(xn, wrow_ref[:, o:o + GLA_QK]) * (GLA_DK ** -0.5)
    o += GLA_QK
    gk_ref[...] = _dot(xn, wrow_ref[:, o:o + GLA_QK])
    o += GLA_QK
    gv_ref[...] = _dot(xn, wrow_ref[:, o:o + GLA_V]).astype(BF16)
    o += GLA_V
    gr_ref[...] = _dot(xn, wrow_ref[:, o:o + GLA_V])
    o += GLA_V
    tm = xn.shape[0]
    for ref in (kc_ref, vc_ref, kw_ref):
        y = _dot(xn, wrow_ref[:, o:o + NSA_KV]).astype(BF16)
        for g in range(NSA_KV_GROUPS):
            ref[g] = y[:, g * NSA_DH:(g + 1) * NSA_DH]
        o += NSA_KV
    y = _dot(xn, wrow_ref[:, o:o + NSA_KV])
    lane = lax.broadcasted_iota(jnp.int32, (tm, NSA_DH), 1)
    blk = lax.broadcasted_iota(jnp.int32, (tm, NSA_DH), 0) // NSA_SEL_BLOCK
    onehot = jnp.where(lane == blk % (SEL_KV_TILE // NSA_SEL_BLOCK), 1.0, 0.0)
    for g in range(NSA_KV_GROUPS):
        ks_ref[g] = jnp.concatenate([y[:, g * NSA_DH:(g + 1) * NSA_DH], onehot], axis=1).astype(BF16)
    o = 0
    gvt_ref[...] = _dot_nt(wt_ref[o:o + GLA_V, :], xn).astype(BF16)
    o += GLA_V
    qt_ref[...] = (_dot_nt(wt_ref[o:o + NSA_Q, :], xn) * (NSA_DH ** -0.5 * LOG2E)).astype(BF16)
    o += NSA_Q
    ones_row = jnp.where(lax.broadcasted_iota(jnp.int32, (NSA_VROWS - NSA_DH, tm), 0) == 0, 1.0, 0.0)
    for ref in (vst_ref, vwt_ref):
        y = _dot_nt(wt_ref[o:o + NSA_KV, :], xn)
        for g in range(NSA_KV_GROUPS):
            ref[g] = jnp.concatenate([y[g * NSA_DH:(g + 1) * NSA_DH], ones_row], axis=0).astype(BF16)
        o += NSA_KV
    ngt_ref[...] = _dot_nt(wt_ref[o:o + 2 * NSA_GATE_ROWS, :], xn)
    lr = _dot(xn, wlr_ref[...])
    l1, l2, l3 = _split3(lr)
    w1, w2, w3 = _split3(wgate_ref[...])
    z = (_dot(l1, w1) + _dot(l1, w2) + _dot(l2, w1) + _dot(l2, w2) + _dot(l1, w3) + _dot(l3, w1)
         + bgate_ref[...])
    glog_ref[...] = jax.nn.log_sigmoid(z) * (1.0 / GLA_GATE_TEMP)


def _even_proj(h, g, w_in, w_gate, b_gate):
    bsz, seq, d = h.shape
    tm = min(ROW_TILE, seq)
    sizes = (GLA_QK, GLA_QK, GLA_V, GLA_GATE_RANK, GLA_V, NSA_Q) + (NSA_KV,) * 6 + (NSA_HEADS * NSA_BRANCHES,)
    offs = [0]
    for s in sizes:
        offs.append(offs[-1] + s)
    col = lambda i: w_in[:, offs[i]:offs[i + 1]]
    gq, gk, gv, glr, gr, nq, kc, vc, ks, vs, kw, vw, ng = [col(i) for i in range(13)]
    wrow = jnp.concatenate([gq, gk, gv, gr, kc, vc, kw, ks], axis=1).astype(BF16)
    ng_g = ng.reshape(d, NSA_KV_GROUPS, NSA_HPG * NSA_BRANCHES)
    ng_g = jnp.pad(ng_g, ((0, 0), (0, 0), (0, NSA_GATE_ROWS - NSA_HPG * NSA_BRANCHES)))
    wt = jnp.concatenate([gv, nq, vs, vw, ng_g.reshape(d, NSA_KV_GROUPS * NSA_GATE_ROWS)], axis=1).T.astype(BF16)
    wlr = jnp.pad(glr, ((0, 0), (0, LANE - GLA_GATE_RANK))).astype(BF16)
    wgate = jnp.pad(w_gate, ((0, LANE - GLA_GATE_RANK), (0, 0)))
    n_t = wt.shape[0]

    row = lambda w: pl.BlockSpec((None, tm, w), lambda b, i: (b, i, 0))
    grp = pl.BlockSpec((None, NSA_KV_GROUPS, tm, NSA_DH), lambda b, i: (b, 0, i, 0))
    tr = lambda w: pl.BlockSpec((None, w, tm), lambda b, i: (b, 0, i))
    grp_ext = pl.BlockSpec((None, NSA_KV_GROUPS, tm, LANE), lambda b, i: (b, 0, i, 0))
    grp_tr = pl.BlockSpec((None, NSA_KV_GROUPS, NSA_VROWS, tm), lambda b, i: (b, 0, 0, i))
    sds = jax.ShapeDtypeStruct
    kv_shape = sds((bsz, NSA_KV_GROUPS, seq, NSA_DH), BF16)
    vt_shape = sds((bsz, NSA_KV_GROUPS, NSA_VROWS, seq), BF16)
    out_shape = (
        sds((bsz, seq, GLA_QK), F32), sds((bsz, seq, GLA_QK), F32), sds((bsz, seq, GLA_V), BF16),
        sds((bsz, GLA_V, seq), BF16), sds((bsz, seq, GLA_V), F32), sds((bsz, seq, GLA_QK), F32),
        kv_shape, kv_shape, sds((bsz, NSA_KV_GROUPS, seq, LANE), BF16), kv_shape,
        sds((bsz, NSA_Q, seq), BF16), vt_shape, vt_shape,
        sds((bsz, NSA_KV_GROUPS * NSA_GATE_ROWS, seq), F32),
    )
    out_specs = (
        row(GLA_QK), row(GLA_QK), row(GLA_V), tr(GLA_V), row(GLA_V), row(GLA_QK),
        grp, grp, grp_ext, grp,
        tr(NSA_Q), grp_tr, grp_tr, tr(NSA_KV_GROUPS * NSA_GATE_ROWS),
    )
    return pl.pallas_call(
        _even_proj_kernel,
        grid=(bsz, seq // tm),
        in_specs=[
            row(d), _const_spec((1, d)), _const_spec(wrow.shape), _const_spec((n_t, d)),
            _const_spec((d, LANE)), _const_spec((LANE, GLA_QK)), _const_spec((1, GLA_QK)),
        ],
        out_specs=out_specs,
        out_shape=out_shape,
        compiler_params=_cparams("parallel", "parallel"),
        name="even_proj",
    )(h, g.reshape(1, d), wrow, wt, wlr, wgate, b_gate.reshape(1, GLA_QK))


def _gla_kernel(q_ref, k_ref, v_ref, vt_ref, r_ref, glog_ref, ng_ref, o_ref, state_ref):
    c = GLA_CHUNK
    ts = q_ref.shape[0]

    @pl.when(pl.program_id(1) == 0)
    def _():
        state_ref[...] = jnp.zeros_like(state_ref)

    c2 = 2 * c
    ri = lax.broadcasted_iota(jnp.int32, (c2, c2), 0)
    ci = lax.broadcasted_iota(jnp.int32, (c2, c2), 1)
    causal = ri >= ci
    tril2 = jnp.where(causal & ((ri < c) | (ci >= c)), 1.0, 0.0).astype(BF16)
    first = lax.broadcasted_iota(jnp.int32, (c2, 1), 0) < c
    state = [state_ref[h] for h in range(GLA_HEADS)]

    for p0 in range(0, ts, c2):
        rows = slice(p0, p0 + c2)
        b = _dot_exact_lhs(tril2, glog_ref[rows, :])
        b_last0, b_last1 = b[c - 1:c, :], b[c2 - 1:c2, :]
        d0, d1 = jnp.exp(b_last0), jnp.exp(b_last1)
        q_t = q_ref[rows, :] * jnp.exp(b)
        k = k_ref[rows, :]
        k_t = (k * jnp.exp(-b)).astype(BF16)
        k_end = k * jnp.exp(jnp.where(first, b_last0, b_last1) - b)
        qa = (q_t * jnp.where(first, 1.0, d0)).astype(BF16)
        qb = jnp.where(first, 0.0, q_t).astype(BF16)
        ka = jnp.where(first, k_t, 0.0)
        kb = jnp.where(first, 0.0, k_t)
        ke = (k_end * jnp.where(first, d1, 1.0)).astype(BF16)
        dd = d0 * d1
        for h in range(GLA_HEADS):
            ks = slice(h * GLA_DK, (h + 1) * GLA_DK)
            vs = slice(h * GLA_DV, (h + 1) * GLA_DV)
            a = _dot_nt(jnp.concatenate([qa[:, ks], qb[:, ks]], axis=1), jnp.concatenate([ka[:, ks], kb[:, ks]], axis=1))
            a = jnp.where(causal, a, 0.0).astype(BF16)
            o = _dot(a, v_ref[rows, vs]) + _dot_nt(qa[:, ks], state[h].astype(BF16))
            state[h] = state[h] * dd[:, ks] + _dot(vt_ref[vs, rows], ke[:, ks])
            o = o * lax.rsqrt(jnp.mean(o * o, axis=-1, keepdims=True) + NORM_EPS) * ng_ref[:, vs]
            o_ref[rows, vs] = o * jax.nn.silu(r_ref[rows, vs])
    for h in range(GLA_HEADS):
        state_ref[h] = state[h]


def _gla(gq, gk, gv, gvt, gr, glog, gla_norm):
    bsz, seq, _ = gq.shape
    ts = min(GLA_ROW_TILE, seq)
    row = lambda w: pl.BlockSpec((None, ts, w), lambda b, i: (b, i, 0))
    return pl.pallas_call(
        _gla_kernel,
        grid=(bsz, seq // ts),
        in_specs=[row(GLA_QK), row(GLA_QK), row(GLA_V),
                  pl.BlockSpec((None, GLA_V, ts), lambda b, i: (b, 0, i)),
                  row(GLA_V), row(GLA_QK), _const_spec((1, GLA_V))],
        out_specs=row(GLA_V),
        out_shape=jax.ShapeDtypeStruct((bsz, seq, GLA_V), F32),
        scratch_shapes=[pltpu.VMEM((GLA_HEADS, GLA_DV, GLA_DK), F32)],
        compiler_params=_cparams("parallel", "arbitrary"),
        name="gla",
    )(gq, gk, gv, gvt, gr, glog, gla_norm.reshape(1, GLA_V))


def _nsa_compress_kernel(kc_ref, vc_ref, pos_ref, w1_ref, w2_ref, w2t_ref, kcmp_ref, vcmpt_ref):
    half = (NSA_CMP_BLOCK // 2) * NSA_DH

    def hidden(x_ref, i):
        x = x_ref[...]
        first = _dot(x, w1_ref[i, :half, :])
        second = _dot(x, w1_ref[i, half:, :])
        nr = first.shape[0]
        nxt = jnp.concatenate([second[1:], second[:1]], axis=0)
        posc = _dot(pos_ref[i], w1_ref[i])[0:1]
        return jax.nn.gelu(first + nxt + posc).astype(BF16)

    kcmp_ref[...] = _dot(hidden(kc_ref, 0), w2_ref[0]).astype(BF16)
    vcmpt_ref[...] = _dot_nt(w2t_ref[...], hidden(vc_ref, 1)).astype(BF16)


def _nsa_compress(kc, vc, cmp_pos, cmp_w1, cmp_w2):
    bsz, g_n, seq, dh = kc.shape
    grp_tok = NSA_CMP_BLOCK // 2
    nr = seq // grp_tok
    feat = grp_tok * dh
    kc_r = kc.reshape(bsz, g_n, nr, feat)
    vc_r = vc.reshape(bsz, g_n, nr, feat)
    pos = jnp.broadcast_to(cmp_pos.reshape(2, 1, NSA_CMP_BLOCK * dh), (2, 8, NSA_CMP_BLOCK * dh)).astype(BF16)
    blk = pl.BlockSpec((None, None, nr, feat), lambda b, g: (b, g, 0, 0))
    return pl.pallas_call(
        _nsa_compress_kernel,
        grid=(bsz, g_n),
        in_specs=[blk, blk, _const_spec(pos.shape), _const_spec(cmp_w1.shape), _const_spec(cmp_w2.shape),
                  _const_spec((dh, NSA_CMP_HIDDEN))],
        out_specs=(pl.BlockSpec((None, None, nr, dh), lambda b, g: (b, g, 0, 0)),
                   pl.BlockSpec((None, None, dh, nr), lambda b, g: (b, g, 0, 0))),
        out_shape=(jax.ShapeDtypeStruct((bsz, g_n, nr, dh), BF16),
                   jax.ShapeDtypeStruct((bsz, g_n, dh, nr), BF16)),
        compiler_params=_cparams("parallel", "parallel"),
        name="nsa_compress",
    )(kc_r, vc_r, pos, cmp_w1.astype(BF16), cmp_w2.astype(BF16), cmp_w2[1].T.astype(BF16))


def _heads_on_lanes(qt_ref, row0=0):
    return jnp.concatenate([qt_ref[row0 + h * NSA_DH:row0 + (h + 1) * NSA_DH, :] for h in range(NSA_HPG)], axis=1)


def _gate_row(gates, h, branch):
    r = h * NSA_BRANCHES + branch
    return gates[r:r + 1, :]


def _nsa_select_kernel(qt_ref, gt_ref, gb_ref, kcmp_ref, vcmpt_ref, ovl_ref, oc_ref, neg_ref):
    dh, hpg = NSA_DH, NSA_HPG
    w = qt_ref.shape[1]
    nr = kcmp_ref.shape[0]
    n_sel = neg_ref.shape[0]
    step = pl.program_id(2)
    hs = lambda h: slice(h * w, (h + 1) * w)

    def select(ns):
        cr = ns * (nr // n_sel)
        q = _heads_on_lanes(qt_ref)
        t = step * w + lax.broadcasted_iota(jnp.int32, (1, w), 1)

        s_c = _dot(kcmp_ref[:cr, :], q)
        cmp_end = lax.broadcasted_iota(jnp.int32, (cr, 1), 0) * NSA_CMP_STRIDE + (NSA_CMP_BLOCK - 1)
        ok_c = cmp_end <= t
        has_key = t >= NSA_CMP_BLOCK - 1
        p_c = []
        for h in range(hpg):
            s_h = jnp.where(ok_c, s_c[:, hs(h)], NEG_BIG)
            e = jnp.exp2(s_h - jnp.max(s_h, axis=0, keepdims=True))
            p_c.append(e * jnp.where(has_key, 1.0 / jnp.sum(e, axis=0, keepdims=True), 0.0))
        o_c = _dot(vcmpt_ref[:, :cr], jnp.concatenate(p_c, axis=1).astype(BF16))

        p1, p2, _ = _split3(sum(p_c[1:], p_c[0]))
        imp = _dot(ovl_ref[:ns, :cr], p1) + _dot(ovl_ref[:ns, :cr], p2)
        j = lax.broadcasted_iota(jnp.int32, (ns, 1), 0)
        jf = j.astype(F32)
        cur = t // NSA_SEL_BLOCK
        forced = (j == 0) | (j == cur) | (j == cur - 1)
        score = jnp.where(forced, -jnp.inf, jnp.where(j <= cur, imp, NEG_BIG))
        for _ in range(min(NSA_TOPN, ns) - 3):
            top = jnp.max(score, axis=0, keepdims=True)
            first = jnp.min(jnp.where(score == top, jf, float(ns)), axis=0, keepdims=True)
            score = jnp.where(jf == first, -jnp.inf, score)
        neg_ref[:ns, :] = jnp.where((score == -jnp.inf) & (j <= cur), 0.0, NEG_BIG)
        if ns < n_sel:
            neg_ref[ns:, :] = jnp.full((n_sel - ns, w), NEG_BIG, F32)

        gates = jax.nn.sigmoid(gt_ref[...] + gb_ref[...])
        for h in range(hpg):
            oc_ref[h * dh:(h + 1) * dh, :] = _gate_row(gates, h, 0) * o_c[:, hs(h)]

    n_var = max(1, min(SEL_VARIANTS, n_sel // NSA_TOPN))
    blocks_per_step = w // NSA_SEL_BLOCK
    lo = 0
    for v in range(1, n_var + 1):
        ns = n_sel * v // n_var
        hi = ns // blocks_per_step
        pl.when((step >= lo) & (step < hi))(functools.partial(select, ns))
        lo = hi


def _nsa_attn_kernel(qt_ref, gt_ref, gb_ref, oc_ref, neg_ref, ks_ref, vst_ref, kw_ref, vwt_ref, o_ref,
                     qext_ref, s_ref, smax_ref, sw_ref):
    qb, dh, hpg, sb = qt_ref.shape[1], NSA_DH, NSA_HPG, NSA_SEL_BLOCK
    assert NSA_KV_GROUPS == 2
    groups = range(NSA_KV_GROUPS)
    qrows = hpg * dh
    n = pl.program_id(1)
    t0 = n * qb
    seq = kw_ref.shape[1]
    hs = lambda h: slice(h * qb, (h + 1) * qb)
    q = [_heads_on_lanes(qt_ref, g * qrows) for g in groups]
    t = t0 + lax.broadcasted_iota(jnp.int32, (1, qb), 1)

    tk = SEL_KV_TILE
    bpt = tk // sb
    mrows = 16
    for g in groups:
        qext_ref[g, 0:dh, :] = q[g]
        qext_ref[g, dh + mrows:, :] = jnp.zeros((qext_ref.shape[1] - dh - mrows, hpg * qb), BF16)
    kpos_tile = lax.broadcasted_iota(jnp.int32, (tk, 1), 0)

    def scores(g, kt):
        rows = neg_ref[g, pl.ds(pl.multiple_of(kt * bpt, bpt), bpt), :]
        ext = jnp.concatenate([rows, jnp.zeros((mrows - bpt, qb), F32)], axis=0).astype(BF16)
        qext_ref[g, dh:dh + mrows, :] = jnp.concatenate([ext] * hpg, axis=1)
        s = _dot(ks_ref[g, pl.ds(pl.multiple_of(kt * tk, tk), tk), :], qext_ref[g])
        s_ref[g] = s
        smax_ref[g] = jnp.max(s, axis=0, keepdims=True)

    def accumulate(g, kt, carry_g, diagonal):
        m, acc = carry_g
        k0 = pl.multiple_of(kt * tk, tk)
        if diagonal:
            ok = kpos_tile + k0 <= t
            s_d = [jnp.where(ok, s_ref[g, :, hs(h)], NEG_BIG) for h in range(hpg)]
            m_new = jnp.maximum(m, jnp.concatenate([jnp.max(s_h, axis=0, keepdims=True) for s_h in s_d], axis=1))
        else:
            m_new = jnp.maximum(m, smax_ref[g])
        p = []
        for h in range(hpg):
            s_h = s_d[h] if diagonal else s_ref[g, :, hs(h)]
            p.append(jnp.exp2(s_h - m_new[:, hs(h)]).astype(BF16))
        acc = jnp.exp2(m - m_new) * acc + _dot(vst_ref[g, :, pl.ds(k0, tk)], jnp.concatenate(p, axis=1))
        return m_new, acc

    def sel_step(kt, carry):
        scores(0, kt)
        c1 = accumulate(1, kt, carry[1], False)
        scores(1, kt + 1)
        c0 = accumulate(0, kt, carry[0], False)
        return c0, c1

    wk = sw_ref.shape[1]
    start = pl.multiple_of(jnp.maximum(t0 + qb - wk, 0), qb)
    kpos = start + lax.broadcasted_iota(jnp.int32, (wk, 1), 0)
    ok_w = (kpos <= t) & (kpos > t - NSA_WINDOW)

    def window_scores(g):
        sw_ref[g] = _dot(kw_ref[g, pl.ds(start, wk), :], q[g])

    def window_out(g):
        p_w = []
        for h in range(hpg):
            s_h = jnp.where(ok_w, sw_ref[g, :, hs(h)], NEG_BIG)
            p_w.append(jnp.exp2(s_h - jnp.max(s_h, axis=0, keepdims=True)).astype(BF16))
        acc_w = _dot(vwt_ref[g, :, pl.ds(start, wk)], jnp.concatenate(p_w, axis=1))
        return acc_w[:dh] / jnp.maximum(acc_w[dh:dh + 1], 1e-30)

    last = (n + tk // qb) // (tk // qb) - 1
    init = (jnp.full((1, hpg * qb), NEG_BIG, F32), jnp.zeros((NSA_VROWS, hpg * qb), F32))
    window_scores(0)
    scores(1, 0)
    o_w = [window_out(0)]
    window_scores(1)
    carry = lax.fori_loop(0, last, sel_step, (init, init))
    scores(0, last)
    o_w.append(window_out(1))
    carry = (carry[0], accumulate(1, last, carry[1], True))
    carry = (accumulate(0, last, carry[0], True), carry[1])

    gates = jax.nn.sigmoid(gt_ref[...] + gb_ref[...])
    for g in groups:
        acc_s = carry[g][1]
        o_s = acc_s[:dh] / jnp.maximum(acc_s[dh:dh + 1], 1e-30)
        gates_g = gates[g * NSA_GATE_ROWS:(g + 1) * NSA_GATE_ROWS]
        for h in range(hpg):
            r = slice(g * qrows + h * dh, g * qrows + (h + 1) * dh)
            o_ref[r, :] = (oc_ref[r, :] + _gate_row(gates_g, h, 1) * o_s[:, hs(h)]
                           + _gate_row(gates_g, h, 2) * o_w[g][:, hs(h)])


def _nsa_attn(qt, ngt, gate_b, kcmp, vcmpt, ks, vst, kw, vwt):
    bsz, g_n, seq, dh = kw.shape
    nr = kcmp.shape[2]
    n_sel = seq // NSA_SEL_BLOCK
    n_cmp = (seq - NSA_CMP_BLOCK) // NSA_CMP_STRIDE + 1
    cmp_start = jnp.arange(nr) * NSA_CMP_STRIDE
    sel_start = jnp.arange(n_sel) * NSA_SEL_BLOCK
    ovl = ((cmp_start[None, :] < sel_start[:, None] + NSA_SEL_BLOCK)
           & (cmp_start[None, :] + NSA_CMP_BLOCK > sel_start[:, None])
           & (jnp.arange(nr)[None, :] < n_cmp)).astype(BF16)
    gb = jnp.pad(gate_b.reshape(g_n, NSA_HPG * NSA_BRANCHES), ((0, 0), (0, NSA_GATE_ROWS - NSA_HPG * NSA_BRANCHES)))
    gb = gb.reshape(g_n, NSA_GATE_ROWS, 1)
    qrows = NSA_HPG * dh
    full = lambda a, c: pl.BlockSpec((None, None, a, c), lambda b, g, n: (b, g, 0, 0))
    cols = lambda r, w: pl.BlockSpec((None, r, w), lambda b, g, n: (b, g, n))
    gb_spec = pl.BlockSpec((None, NSA_GATE_ROWS, 1), lambda b, g, n: (g, 0, 0))
    sw = min(SEL_QBLOCKS * NSA_QBLOCK, seq)
    oc, neg = pl.pallas_call(
        _nsa_select_kernel,
        grid=(bsz, g_n, seq // sw),
        in_specs=[cols(qrows, sw), cols(NSA_GATE_ROWS, sw), gb_spec,
                  full(nr, dh), full(dh, nr), _const_spec((n_sel, nr))],
        out_specs=(cols(qrows, sw), pl.BlockSpec((None, None, n_sel, sw), lambda b, g, n: (b, g, 0, n))),
        out_shape=(jax.ShapeDtypeStruct((bsz, g_n * qrows, seq), F32),
                   jax.ShapeDtypeStruct((bsz, g_n, n_sel, seq), F32)),
        compiler_params=_cparams("parallel", "parallel", "parallel"),
        name="nsa_select",
    )(qt, ngt, gb, kcmp, vcmpt, ovl)
    qb = min(ATTN_QBLOCK, seq)
    n_qb = seq // qb
    qcols = lambda r: pl.BlockSpec((None, r, qb), lambda b, n: (b, 0, n))
    whole = lambda a, c: pl.BlockSpec((None, g_n, a, c), lambda b, n: (b, 0, 0, 0))
    return pl.pallas_call(
        _nsa_attn_kernel,
        grid=(bsz, n_qb),
        in_specs=[qcols(g_n * qrows), qcols(g_n * NSA_GATE_ROWS), _const_spec((g_n * NSA_GATE_ROWS, 1)),
                  qcols(g_n * qrows), pl.BlockSpec((None, g_n, n_sel, qb), lambda b, n: (b, 0, 0, n)),
                  whole(seq, LANE), whole(NSA_VROWS, seq), whole(seq, dh), whole(NSA_VROWS, seq)],
        out_specs=qcols(g_n * qrows),
        out_shape=jax.ShapeDtypeStruct((bsz, g_n * qrows, seq), F32),
        scratch_shapes=[pltpu.VMEM((g_n, LANE, NSA_HPG * qb), BF16),
                        pltpu.VMEM((g_n, SEL_KV_TILE, NSA_HPG * qb), F32),
                        pltpu.VMEM((g_n, 1, NSA_HPG * qb), F32),
                        pltpu.VMEM((g_n, min(NSA_WINDOW + qb, seq), NSA_HPG * qb), F32)],
        compiler_params=_cparams("parallel", "parallel"),
        name="nsa_attn",
    )(qt, ngt, gb.reshape(g_n * NSA_GATE_ROWS, 1), oc, neg, ks, vst, kw, vwt)


def _ffn_residual(h, g_in_ref, w1_ref, w2_ref, g_out_ref):
    xn = _rms(h, g_in_ref[...]).astype(BF16)
    hid = w1_ref.shape[1]
    step = 1024
    acc = jnp.zeros(h.shape, F32)
    for j in range(0, hid, step):
        a = jnp.maximum(_dot(xn, w1_ref[:, j:j + step]), 0.0)
        acc = acc + _dot((a * a).astype(BF16), w2_ref[j:j + step, :])
    return h + _rms(acc, g_out_ref[...])


def _even_out_ffn_kernel(h_ref, oa_ref, obt_ref, w_ref, g_ref, g_in_ref, w1_ref, w2_ref, g_out_ref, o_ref):
    m = _dot(oa_ref[...].astype(BF16), w_ref[:GLA_V, :]) + lax.dot_general(
        obt_ref[...].astype(BF16), w_ref[GLA_V:, :], (((0,), (0,)), ((), ())), preferred_element_type=F32)
    h = h_ref[...] + _rms(m, g_ref[...])
    o_ref[...] = _ffn_residual(h, g_in_ref, w1_ref, w2_ref, g_out_ref)


def _even_out_ffn(h, o_a, o_bt, w_out, g, g_in, w1, w2, g_out):
    bsz, seq, d = h.shape
    tm = min(ROW_TILE, seq)
    row = lambda w: pl.BlockSpec((None, tm, w), lambda b, i: (b, i, 0))
    return pl.pallas_call(
        _even_out_ffn_kernel,
        grid=(bsz, seq // tm),
        in_specs=[row(d), row(GLA_V), pl.BlockSpec((None, NSA_Q, tm), lambda b, i: (b, 0, i)),
                  _const_spec(w_out.shape), _const_spec((1, d)),
                  _const_spec((1, d)), _const_spec(w1.shape), _const_spec(w2.shape), _const_spec((1, d))],
        out_specs=row(d),
        out_shape=jax.ShapeDtypeStruct(h.shape, F32),
        compiler_params=_cparams("parallel", "parallel"),
        name="even_out_ffn",
    )(h, o_a, o_bt, w_out.astype(BF16), g.reshape(1, d),
      g_in.reshape(1, d), w1.astype(BF16), w2.astype(BF16), g_out.reshape(1, d))


def _ffn_kernel(h_ref, g_in_ref, w1_ref, w2_ref, g_out_ref, o_ref):
    o_ref[...] = _ffn_residual(h_ref[...], g_in_ref, w1_ref, w2_ref, g_out_ref)


def _ffn(h, g_in, w1, w2, g_out):
    bsz, seq, d = h.shape
    tm = min(ROW_TILE, seq)
    row = pl.BlockSpec((None, tm, d), lambda b, i: (b, i, 0))
    return pl.pallas_call(
        _ffn_kernel,
        grid=(bsz, seq // tm),
        in_specs=[row, _const_spec((1, d)), _const_spec(w1.shape), _const_spec(w2.shape), _const_spec((1, d))],
        out_specs=row,
        out_shape=jax.ShapeDtypeStruct(h.shape, F32),
        compiler_params=_cparams("parallel", "parallel"),
        name="ffn",
    )(h, g_in.reshape(1, d), w1.astype(BF16), w2.astype(BF16), g_out.reshape(1, d))


def _sgu_kernel(h_ref, g_in_ref, w_in_ref, ln_g_ref, ln_b_ref, ws_ref, bs_ref, w_out_ref, g_out_ref, o_ref):
    e, c, gw = SGU_WIDTH, SGU_CHUNK, SGU_WIDTH // SGU_GROUPS
    h = h_ref[...]
    tm = h.shape[0]
    xn = _rms(h, g_in_ref[...]).astype(BF16)
    u = jax.nn.gelu(_dot(xn, w_in_ref[:, :e]))
    v = jax.nn.gelu(_dot(xn, w_in_ref[:, e:]))
    mu = jnp.mean(v, axis=-1, keepdims=True)
    var = jnp.mean((v - mu) ** 2, axis=-1, keepdims=True)
    vn = ((v - mu) * lax.rsqrt(var + NORM_EPS) * ln_g_ref[...] + ln_b_ref[...]).astype(BF16)
    causal = lax.broadcasted_iota(jnp.int32, (c, c), 0) >= lax.broadcasted_iota(jnp.int32, (c, c), 1)
    w_c = [jnp.where(causal, ws_ref[g], 0.0).astype(BF16) for g in range(SGU_GROUPS)]
    rows = []
    for r0 in range(0, tm, c):
        mixed = [_dot(w_c[g], vn[r0:r0 + c, g * gw:(g + 1) * gw]) for g in range(SGU_GROUPS)]
        rows.append(jnp.concatenate(mixed, axis=1) + bs_ref[...])
    y = (u * jnp.concatenate(rows, axis=0)).astype(BF16)
    o_ref[...] = h + _rms(_dot(y, w_out_ref[...]), g_out_ref[...])


def _sgu(h, g_in, w_in, ln_g, ln_b, w_s, b_s, w_out, g_out):
    bsz, seq, d = h.shape
    tm = min(SGU_ROW_TILE, seq)
    e = SGU_WIDTH
    bias = jnp.repeat(b_s.T, e // SGU_GROUPS, axis=1)
    row = pl.BlockSpec((None, tm, d), lambda b, i: (b, i, 0))
    return pl.pallas_call(
        _sgu_kernel,
        grid=(bsz, seq // tm),
        in_specs=[row, _const_spec((1, d)), _const_spec(w_in.shape), _const_spec((1, e)), _const_spec((1, e)),
                  _const_spec(w_s.shape), _const_spec(bias.shape), _const_spec(w_out.shape), _const_spec((1, d))],
        out_specs=row,
        out_shape=jax.ShapeDtypeStruct(h.shape, F32),
        compiler_params=_cparams("parallel", "parallel"),
        name="sgu",
    )(h, g_in.reshape(1, d), w_in.astype(BF16), ln_g.reshape(1, e), ln_b.reshape(1, e), w_s, bias,
      w_out.astype(BF16), g_out.reshape(1, d))


def _even_mixer_outputs(h, g_in, w_in, w_gate, b_gate, gla_norm, gate_b, cmp_pos, cmp_w1, cmp_w2):
    (gq, gk, gv, gvt, gr, glog, kc, vc, ks, kw, qt, vst, vwt, ngt) = _even_proj(h, g_in, w_in, w_gate, b_gate)
    o_a = _gla(gq, gk, gv, gvt, gr, glog, gla_norm)
    kcmp, vcmpt = _nsa_compress(kc, vc, cmp_pos, cmp_w1, cmp_w2)
    return o_a, _nsa_attn(qt, ngt, gate_b, kcmp, vcmpt, ks, vst, kw, vwt)


def kernel(x, norm_g, ffn_w1, ffn_w2, e_w_in, e_w_out, gla_w_gate, gla_b_gate, gla_norm, nsa_gate_b, nsa_cmp_pos, nsa_cmp_w1, nsa_cmp_w2, o_w_in, o_ln_g, o_ln_b, o_w_s, o_b_s, o_w_out):
    h = x
    depth = norm_g.shape[0]
    for layer in range(depth):
        i = layer // 2
        ffn_args = (norm_g[layer, 2], ffn_w1[layer], ffn_w2[layer], norm_g[layer, 3])
        if layer % 2 == 0:
            o_a, o_bt = _even_mixer_outputs(h, norm_g[layer, 0], e_w_in[i], gla_w_gate[i], gla_b_gate[i],
                                            gla_norm[i], nsa_gate_b[i], nsa_cmp_pos[i], nsa_cmp_w1[i], nsa_cmp_w2[i])
            h = _even_out_ffn(h, o_a, o_bt, e_w_out[i], norm_g[layer, 1], *ffn_args)
        else:
            h = _sgu(h, norm_g[layer, 0], o_w_in[i], o_ln_g[i], o_ln_b[i], o_w_s[i], o_b_s[i], o_w_out[i],
                     norm_g[layer, 1])
            h = _ffn(h, *ffn_args)
    return h
```

```python
import functools

import jax
import jax.numpy as jnp
from jax import lax
from jax.experimental import pallas as pl
from jax.experimental.pallas import tpu as pltpu

F32 = jnp.float32
BF16 = jnp.bfloat16

D_MODEL = 1024
GLA_HEADS = 4
GLA_DK = 64
GLA_DV = 128
GLA_GATE_RANK = 16
GLA_GATE_TEMP = 16.0
GLA_CHUNK = 64
NSA_HEADS = 8
NSA_KV_GROUPS = 2
NSA_HPG = NSA_HEADS // NSA_KV_GROUPS
NSA_DH = 64
NSA_CMP_BLOCK = 32
NSA_CMP_STRIDE = 16
NSA_CMP_HIDDEN = 128
NSA_SEL_BLOCK = 64
NSA_TOPN = 16
NSA_WINDOW = 512
NSA_QBLOCK = 128
NSA_BRANCHES = 3
SGU_WIDTH = 2 * D_MODEL
SGU_GROUPS = 8
SGU_CHUNK = 128
FFN_HIDDEN = 4 * D_MODEL
NORM_EPS = 1e-6
NEG_BIG = -1e30
POS_BIG = 1e30

GLA_QK = GLA_HEADS * GLA_DK
GLA_V = GLA_HEADS * GLA_DV
NSA_Q = NSA_HEADS * NSA_DH
NSA_KV = NSA_KV_GROUPS * NSA_DH
NSA_GATE_ROWS = 16
NSA_VROWS = NSA_DH + 16
LOG2E = 1.4426950408889634
LANE = 128

VMEM_LIMIT_BYTES = 56 * 1024 * 1024

ROW_TILE = 512
SGU_ROW_TILE = 512
GLA_ROW_TILE = 256
SEL_KV_TILE = 512
ATTN_QBLOCK = 256
SEL_VARIANTS = 4
SEL_QBLOCKS = 4


def _cparams(*sem):
    return pltpu.CompilerParams(dimension_semantics=sem, vmem_limit_bytes=VMEM_LIMIT_BYTES)


def _rms(x, g):
    return x * lax.rsqrt(jnp.mean(x * x, axis=-1, keepdims=True) + NORM_EPS) * g


def _dot(a, b):
    return jnp.dot(a, b, preferred_element_type=F32)


def _dot_nt(a, b):
    return lax.dot_general(a, b, (((1,), (1,)), ((), ())), preferred_element_type=F32)


def _split3(x):
    x1 = x.astype(BF16)
    r1 = x - x1.astype(F32)
    x2 = r1.astype(BF16)
    r2 = r1 - x2.astype(F32)
    return x1, x2, r2.astype(BF16)


def _dot_exact_lhs(a_bf16, b_f32):
    b1, b2, b3 = _split3(b_f32)
    return _dot(a_bf16, b1) + _dot(a_bf16, b2) + _dot(a_bf16, b3)


def _const_spec(shape):
    return pl.BlockSpec(shape, lambda *_: (0,) * len(shape))


def _even_proj_kernel(h_ref, g_ref, wrow_ref, wt_ref, wlr_ref, wgate_ref, bgate_ref,
                      gq_ref, gk_ref, gv_ref, gvt_ref, gr_ref, glog_ref,
                      kc_ref, vc_ref, ks_ref, kw_ref, qt_ref, vst_ref, vwt_ref, ngt_ref):
    tm_all = h_ref.shape[0]
    tm = tm_all // 2
    w1, w2, w3 = _split3(wgate_ref[...])
    lane = lax.broadcasted_iota(jnp.int32, (tm, NSA_DH), 1)
    blk = lax.broadcasted_iota(jnp.int32, (tm, NSA_DH), 0) // NSA_SEL_BLOCK
    ones_row = jnp.where(lax.broadcasted_iota(jnp.int32, (NSA_VROWS - NSA_DH, tm), 0) == 0, 1.0, 0.0)

    def row_major(r0, xn):
        rows = slice(r0, r0 + tm)
        o = 0
        gq_ref[rows, :] = _dot(xn, wrow_ref[:, o:o + GLA_QK]) * (GLA_DK ** -0.5)
        o += GLA_QK
        gk_ref[rows, :] = _dot(xn, wrow_ref[:, o:o + GLA_QK])
        o += GLA_QK
        gv_ref[rows, :] = _dot(xn, wrow_ref[:, o:o + GLA_V]).astype(BF16)
        o += GLA_V
        gr_ref[rows, :] = _dot(xn, wrow_ref[:, o:o + GLA_V])
        o += GLA_V
        for ref in (kc_ref, vc_ref, kw_ref):
            y = _dot(xn, wrow_ref[:, o:o + NSA_KV]).astype(BF16)
            for g in range(NSA_KV_GROUPS):
                ref[g, rows, :] = y[:, g * NSA_DH:(g + 1) * NSA_DH]
            o += NSA_KV
        y = _dot(xn, wrow_ref[:, o:o + NSA_KV])
        onehot = jnp.where(lane == (blk + r0 // NSA_SEL_BLOCK) % (SEL_KV_TILE // NSA_SEL_BLOCK), 1.0, 0.0)
        for g in range(NSA_KV_GROUPS):
            ks_ref[g, rows, :] = jnp.concatenate([y[:, g * NSA_DH:(g + 1) * NSA_DH], onehot], axis=1).astype(BF16)
        lr = _dot(xn, wlr_ref[...])
        l1, l2, l3 = _split3(lr)
        z = (_dot(l1, w1) + _dot(l1, w2) + _dot(l2, w1) + _dot(l2, w2) + _dot(l1, w3) + _dot(l3, w1)
             + bgate_ref[...])
        glog_ref[rows, :] = jax.nn.log_sigmoid(z) * (1.0 / GLA_GATE_TEMP)

    def transposed(r0, xn):
        cols = slice(r0, r0 + tm)
        o = 0
        gvt_ref[:, cols] = _dot_nt(wt_ref[o:o + GLA_V, :], xn).astype(BF16)
        o += GLA_V
        qt_ref[:, cols] = (_dot_nt(wt_ref[o:o + NSA_Q, :], xn) * (NSA_DH ** -0.5 * LOG2E)).astype(BF16)
        o += NSA_Q
        for ref in (vst_ref, vwt_ref):
            y = _dot_nt(wt_ref[o:o + NSA_KV, :], xn)
            for g in range(NSA_KV_GROUPS):
                ref[g, :, cols] = jnp.concatenate([y[g * NSA_DH:(g + 1) * NSA_DH], ones_row], axis=0).astype(BF16)
            o += NSA_KV
        ngt_ref[:, cols] = _dot_nt(wt_ref[o:o + 2 * NSA_GATE_ROWS, :], xn)

    xn_a = _rms(h_ref[0:tm, :], g_ref[...]).astype(BF16)
    xn_b = _rms(h_ref[tm:tm_all, :], g_ref[...]).astype(BF16)
    row_major(0, xn_a)
    transposed(0, xn_a)
    row_major(tm, xn_b)
    transposed(tm, xn_b)


def _even_proj(h, g, w_in, w_gate, b_gate):
    bsz, seq, d = h.shape
    tm = min(ROW_TILE, seq)
    sizes = (GLA_QK, GLA_QK, GLA_V, GLA_GATE_RANK, GLA_V, NSA_Q) + (NSA_KV,) * 6 + (NSA_HEADS * NSA_BRANCHES,)
    offs = [0]
    for s in sizes:
        offs.append(offs[-1] + s)
    col = lambda i: w_in[:, offs[i]:offs[i + 1]]
    gq, gk, gv, glr, gr, nq, kc, vc, ks, vs, kw, vw, ng = [col(i) for i in range(13)]
    wrow = jnp.concatenate([gq, gk, gv, gr, kc, vc, kw, ks], axis=1).astype(BF16)
    ng_g = ng.reshape(d, NSA_KV_GROUPS, NSA_HPG * NSA_BRANCHES)
    ng_g = jnp.pad(ng_g, ((0, 0), (0, 0), (0, NSA_GATE_ROWS - NSA_HPG * NSA_BRANCHES)))
    wt = jnp.concatenate([gv, nq, vs, vw, ng_g.reshape(d, NSA_KV_GROUPS * NSA_GATE_ROWS)], axis=1).T.astype(BF16)
    wlr = jnp.pad(glr, ((0, 0), (0, LANE - GLA_GATE_RANK))).astype(BF16)
    wgate = jnp.pad(w_gate, ((0, LANE - GLA_GATE_RANK), (0, 0)))
    n_t = wt.shape[0]

    row = lambda w: pl.BlockSpec((None, tm, w), lambda b, i: (b, i, 0))
    grp = pl.BlockSpec((None, NSA_KV_GROUPS, tm, NSA_DH), lambda b, i: (b, 0, i, 0))
    tr = lambda w: pl.BlockSpec((None, w, tm), lambda b, i: (b, 0, i))
    grp_ext = pl.BlockSpec((None, NSA_KV_GROUPS, tm, LANE), lambda b, i: (b, 0, i, 0))
    grp_tr = pl.BlockSpec((None, NSA_KV_GROUPS, NSA_VROWS, tm), lambda b, i: (b, 0, 0, i))
    sds = jax.ShapeDtypeStruct
    kv_shape = sds((bsz, NSA_KV_GROUPS, seq, NSA_DH), BF16)
    vt_shape = sds((bsz, NSA_KV_GROUPS, NSA_VROWS, seq), BF16)
    out_shape = (
        sds((bsz, seq, GLA_QK), F32), sds((bsz, seq, GLA_QK), F32), sds((bsz, seq, GLA_V), BF16),
        sds((bsz, GLA_V, seq), BF16), sds((bsz, seq, GLA_V), F32), sds((bsz, seq, GLA_QK), F32),
        kv_shape, kv_shape, sds((bsz, NSA_KV_GROUPS, seq, LANE), BF16), kv_shape,
        sds((bsz, NSA_Q, seq), BF16), vt_shape, vt_shape,
        sds((bsz, NSA_KV_GROUPS * NSA_GATE_ROWS, seq), F32),
    )
    out_specs = (
        row(GLA_QK), row(GLA_QK), row(GLA_V), tr(GLA_V), row(GLA_V), row(GLA_QK),
        grp, grp, grp_ext, grp,
        tr(NSA_Q), grp_tr, grp_tr, tr(NSA_KV_GROUPS * NSA_GATE_ROWS),
    )
    return pl.pallas_call(
        _even_proj_kernel,
        grid=(bsz, seq // tm),
        in_specs=[
            row(d), _const_spec((1, d)), _const_spec(wrow.shape), _const_spec((n_t, d)),
            _const_spec((d, LANE)), _const_spec((LANE, GLA_QK)), _const_spec((1, GLA_QK)),
        ],
        out_specs=out_specs,
        out_shape=out_shape,
        compiler_params=_cparams("parallel", "parallel"),
        name="even_proj",
    )(h, g.reshape(1, d), wrow, wt, wlr, wgate, b_gate.reshape(1, GLA_QK))


def _gla_kernel(q_ref, k_ref, v_ref, vt_ref, r_ref, glog_ref, ng_ref, o_ref, state_ref):
    c = GLA_CHUNK
    ts = q_ref.shape[0]

    @pl.when(pl.program_id(1) == 0)
    def _():
        state_ref[...] = jnp.zeros_like(state_ref)

    c2 = 2 * c
    ri = lax.broadcasted_iota(jnp.int32, (c2, c2), 0)
    ci = lax.broadcasted_iota(jnp.int32, (c2, c2), 1)
    causal = ri >= ci
    tril2 = jnp.where(causal & ((ri < c) | (ci >= c)), 1.0, 0.0).astype(BF16)
    first = lax.broadcasted_iota(jnp.int32, (c2, 1), 0) < c
    state = [state_ref[h] for h in range(GLA_HEADS)]

    for p0 in range(0, ts, c2):
        rows = slice(p0, p0 + c2)
        b = _dot_exact_lhs(tril2, glog_ref[rows, :])
        b_last0, b_last1 = b[c - 1:c, :], b[c2 - 1:c2, :]
        d0, d1 = jnp.exp(b_last0), jnp.exp(b_last1)
        q_t = q_ref[rows, :] * jnp.exp(b)
        k = k_ref[rows, :]
        k_t = (k * jnp.exp(-b)).astype(BF16)
        k_end = k * jnp.exp(jnp.where(first, b_last0, b_last1) - b)
        qa = (q_t * jnp.where(first, 1.0, d0)).astype(BF16)
        qb = jnp.where(first, 0.0, q_t).astype(BF16)
        ka = jnp.where(first, k_t, 0.0)
        kb = jnp.where(first, 0.0, k_t)
        ke = (k_end * jnp.where(first, d1, 1.0)).astype(BF16)
        dd = d0 * d1
        for h in range(GLA_HEADS):
            ks = slice(h * GLA_DK, (h + 1) * GLA_DK)
            vs = slice(h * GLA_DV, (h + 1) * GLA_DV)
            a = _dot_nt(jnp.concatenate([qa[:, ks], qb[:, ks]], axis=1), jnp.concatenate([ka[:, ks], kb[:, ks]], axis=1))
            a = jnp.where(causal, a, 0.0).astype(BF16)
            o = _dot(a, v_ref[rows, vs]) + _dot_nt(qa[:, ks], state[h].astype(BF16))
            state[h] = state[h] * dd[:, ks] + _dot(vt_ref[vs, rows], ke[:, ks])
            o = o * lax.rsqrt(jnp.mean(o * o, axis=-1, keepdims=True) + NORM_EPS) * ng_ref[:, vs]
            o_ref[rows, vs] = o * jax.nn.silu(r_ref[rows, vs])
    for h in range(GLA_HEADS):
        state_ref[h] = state[h]


def _gla(gq, gk, gv, gvt, gr, glog, gla_norm):
    bsz, seq, _ = gq.shape
    ts = min(GLA_ROW_TILE, seq)
    row = lambda w: pl.BlockSpec((None, ts, w), lambda b, i: (b, i, 0))
    return pl.pallas_call(
        _gla_kernel,
        grid=(bsz, seq // ts),
        in_specs=[row(GLA_QK), row(GLA_QK), row(GLA_V),
                  pl.BlockSpec((None, GLA_V, ts), lambda b, i: (b, 0, i)),
                  row(GLA_V), row(GLA_QK), _const_spec((1, GLA_V))],
        out_specs=row(GLA_V),
        out_shape=jax.ShapeDtypeStruct((bsz, seq, GLA_V), F32),
        scratch_shapes=[pltpu.VMEM((GLA_HEADS, GLA_DV, GLA_DK), F32)],
        compiler_params=_cparams("parallel", "arbitrary"),
        name="gla",
    )(gq, gk, gv, gvt, gr, glog, gla_norm.reshape(1, GLA_V))


def _nsa_compress_kernel(kc_ref, vc_ref, pos_ref, w1_ref, w2_ref, w2t_ref, kcmp_ref, vcmpt_ref):
    half = (NSA_CMP_BLOCK // 2) * NSA_DH

    def hidden(x_ref, i):
        x = x_ref[...]
        first = _dot(x, w1_ref[i, :half, :])
        second = _dot(x, w1_ref[i, half:, :])
        nr = first.shape[0]
        nxt = jnp.concatenate([second[1:], second[:1]], axis=0)
        posc = _dot(pos_ref[i], w1_ref[i])[0:1]
        return jax.nn.gelu(first + nxt + posc).astype(BF16)

    kcmp_ref[...] = _dot(hidden(kc_ref, 0), w2_ref[0]).astype(BF16)
    vcmpt_ref[...] = _dot_nt(w2t_ref[...], hidden(vc_ref, 1)).astype(BF16)


def _nsa_compress(kc, vc, cmp_pos, cmp_w1, cmp_w2):
    bsz, g_n, seq, dh = kc.shape
    grp_tok = NSA_CMP_BLOCK // 2
    nr = seq // grp_tok
    feat = grp_tok * dh
    kc_r = kc.reshape(bsz, g_n, nr, feat)
    vc_r = vc.reshape(bsz, g_n, nr, feat)
    pos = jnp.broadcast_to(cmp_pos.reshape(2, 1, NSA_CMP_BLOCK * dh), (2, 8, NSA_CMP_BLOCK * dh)).astype(BF16)
    blk = pl.BlockSpec((None, None, nr, feat), lambda b, g: (b, g, 0, 0))
    return pl.pallas_call(
        _nsa_compress_kernel,
        grid=(bsz, g_n),
        in_specs=[blk, blk, _const_spec(pos.shape), _const_spec(cmp_w1.shape), _const_spec(cmp_w2.shape),
                  _const_spec((dh, NSA_CMP_HIDDEN))],
        out_specs=(pl.BlockSpec((None, None, nr, dh), lambda b, g: (b, g, 0, 0)),
                   pl.BlockSpec((None, None, dh, nr), lambda b, g: (b, g, 0, 0))),
        out_shape=(jax.ShapeDtypeStruct((bsz, g_n, nr, dh), BF16),
                   jax.ShapeDtypeStruct((bsz, g_n, dh, nr), BF16)),
        compiler_params=_cparams("parallel", "parallel"),
        name="nsa_compress",
    )(kc_r, vc_r, pos, cmp_w1.astype(BF16), cmp_w2.astype(BF16), cmp_w2[1].T.astype(BF16))


def _heads_on_lanes(qt_ref, row0=0):
    return jnp.concatenate([qt_ref[row0 + h * NSA_DH:row0 + (h + 1) * NSA_DH, :] for h in range(NSA_HPG)], axis=1)


def _gate_row(gates, h, branch):
    r = h * NSA_BRANCHES + branch
    return gates[r:r + 1, :]


def _nsa_select_kernel(qt_ref, gt_ref, gb_ref, kcmp_ref, vcmpt_ref, ovl_ref, oc_ref, neg_ref):
    dh, hpg = NSA_DH, NSA_HPG
    w = qt_ref.shape[1]
    nr = kcmp_ref.shape[0]
    n_sel = neg_ref.shape[0]
    step = pl.program_id(2)
    hs = lambda h: slice(h * w, (h + 1) * w)

    def select(ns):
        cr = ns * (nr // n_sel)
        q = _heads_on_lanes(qt_ref)
        t = step * w + lax.broadcasted_iota(jnp.int32, (1, w), 1)

        s_c = _dot(kcmp_ref[:cr, :], q)
        cmp_end = lax.broadcasted_iota(jnp.int32, (cr, 1), 0) * NSA_CMP_STRIDE + (NSA_CMP_BLOCK - 1)
        ok_c = cmp_end <= t
        has_key = t >= NSA_CMP_BLOCK - 1
        p_c = []
        for h in range(hpg):
            s_h = jnp.where(ok_c, s_c[:, hs(h)], NEG_BIG)
            e = jnp.exp2(s_h - jnp.max(s_h, axis=0, keepdims=True))
            p_c.append(e * jnp.where(has_key, 1.0 / jnp.sum(e, axis=0, keepdims=True), 0.0))
        o_c = _dot(vcmpt_ref[:, :cr], jnp.concatenate(p_c, axis=1).astype(BF16))

        p1, p2, _ = _split3(sum(p_c[1:], p_c[0]))
        imp = _dot(ovl_ref[:ns, :cr], p1) + _dot(ovl_ref[:ns, :cr], p2)
        j = lax.broadcasted_iota(jnp.int32, (ns, 1), 0)
        jf = j.astype(F32)
        cur = t // NSA_SEL_BLOCK
        forced = (j == 0) | (j == cur) | (j == cur - 1)
        score = jnp.where(forced, -jnp.inf, jnp.where(j <= cur, imp, NEG_BIG))
        for _ in range(min(NSA_TOPN, ns) - 3):
            top = jnp.max(score, axis=0, keepdims=True)
            first = jnp.min(jnp.where(score == top, jf, float(ns)), axis=0, keepdims=True)
            score = jnp.where(jf == first, -jnp.inf, score)
        neg_ref[:ns, :] = jnp.where((score == -jnp.inf) & (j <= cur), 0.0, NEG_BIG)
        if ns < n_sel:
            neg_ref[ns:, :] = jnp.full((n_sel - ns, w), NEG_BIG, F32)

        gates = jax.nn.sigmoid(gt_ref[...] + gb_ref[...])
        for h in range(hpg):
            oc_ref[h * dh:(h + 1) * dh, :] = _gate_row(gates, h, 0) * o_c[:, hs(h)]

    n_var = max(1, min(SEL_VARIANTS, n_sel // NSA_TOPN))
    blocks_per_step = w // NSA_SEL_BLOCK
    lo = 0
    for v in range(1, n_var + 1):
        ns = n_sel * v // n_var
        hi = ns // blocks_per_step
        pl.when((step >= lo) & (step < hi))(functools.partial(select, ns))
        lo = hi


def _nsa_attn_kernel(qt_ref, gt_ref, gb_ref, oc_ref, neg_ref, ks_ref, vst_ref, kw_ref, vwt_ref, o_ref,
                     qext_ref, s_ref, smax_ref, sw_ref):
    qb, dh, hpg, sb = qt_ref.shape[1], NSA_DH, NSA_HPG, NSA_SEL_BLOCK
    assert NSA_KV_GROUPS == 2
    groups = range(NSA_KV_GROUPS)
    qrows = hpg * dh
    n = pl.program_id(1)
    t0 = n * qb
    seq = kw_ref.shape[1]
    hs = lambda h: slice(h * qb, (h + 1) * qb)
    q = [_heads_on_lanes(qt_ref, g * qrows) for g in groups]
    t = t0 + lax.broadcasted_iota(jnp.int32, (1, qb), 1)

    tk = SEL_KV_TILE
    bpt = tk // sb
    mrows = 16
    for g in groups:
        qext_ref[g, 0:dh, :] = q[g]
        qext_ref[g, dh + mrows:, :] = jnp.zeros((qext_ref.shape[1] - dh - mrows, hpg * qb), BF16)
    kpos_tile = lax.broadcasted_iota(jnp.int32, (tk, 1), 0)

    def scores(g, kt):
        rows = neg_ref[g, pl.ds(pl.multiple_of(kt * bpt, bpt), bpt), :]
        ext = jnp.concatenate([rows, jnp.zeros((mrows - bpt, qb), F32)], axis=0).astype(BF16)
        qext_ref[g, dh:dh + mrows, :] = jnp.concatenate([ext] * hpg, axis=1)
        s = _dot(ks_ref[g, pl.ds(pl.multiple_of(kt * tk, tk), tk), :], qext_ref[g])
        s_ref[g] = s
        smax_ref[g] = jnp.max(s, axis=0, keepdims=True)

    def accumulate(g, kt, carry_g, diagonal):
        m, acc = carry_g
        k0 = pl.multiple_of(kt * tk, tk)
        if diagonal:
            ok = kpos_tile + k0 <= t
            s_d = [jnp.where(ok, s_ref[g, :, hs(h)], NEG_BIG) for h in range(hpg)]
            m_new = jnp.maximum(m, jnp.concatenate([jnp.max(s_h, axis=0, keepdims=True) for s_h in s_d], axis=1))
        else:
            m_new = jnp.maximum(m, smax_ref[g])
        p = []
        for h in range(hpg):
            s_h = s_d[h] if diagonal else s_ref[g, :, hs(h)]
            p.append(jnp.exp2(s_h - m_new[:, hs(h)]).astype(BF16))
        acc = jnp.exp2(m - m_new) * acc + _dot(vst_ref[g, :, pl.ds(k0, tk)], jnp.concatenate(p, axis=1))
        return m_new, acc

    def sel_step(kt, carry):
        scores(0, kt)
        c1 = accumulate(1, kt, carry[1], False)
        scores(1, kt + 1)
        c0 = accumulate(0, kt, carry[0], False)
        return c0, c1

    wk = sw_ref.shape[1]
    start = pl.multiple_of(jnp.maximum(t0 + qb - wk, 0), qb)
    kpos = start + lax.broadcasted_iota(jnp.int32, (wk, 1), 0)
    ok_w = (kpos <= t) & (kpos > t - NSA_WINDOW)

    def window_scores(g):
        sw_ref[g] = _dot(kw_ref[g, pl.ds(start, wk), :], q[g])

    def window_out(g):
        p_w = []
        for h in range(hpg):
            s_h = jnp.where(ok_w, sw_ref[g, :, hs(h)], NEG_BIG)
            p_w.append(jnp.exp2(s_h - jnp.max(s_h, axis=0, keepdims=True)).astype(BF16))
        acc_w = _dot(vwt_ref[g, :, pl.ds(start, wk)], jnp.concatenate(p_w, axis=1))
        return acc_w[:dh] / jnp.maximum(acc_w[dh:dh + 1], 1e-30)

    last = (n + tk // qb) // (tk // qb) - 1
    init = (jnp.full((1, hpg * qb), NEG_BIG, F32), jnp.zeros((NSA_VROWS, hpg * qb), F32))
    window_scores(0)
    scores(1, 0)
    o_w = [window_out(0)]
    window_scores(1)
    carry = lax.fori_loop(0, last, sel_step, (init, init))
    scores(0, last)
    o_w.append(window_out(1))
    carry = (carry[0], accumulate(1, last, carry[1], True))
    carry = (accumulate(0, last, carry[0], True), carry[1])

    gates = jax.nn.sigmoid(gt_ref[...] + gb_ref[...])
    for g in groups:
        acc_s = carry[g][1]
        o_s = acc_s[:dh] / jnp.maximum(acc_s[dh:dh + 1], 1e-30)
        gates_g = gates[g * NSA_GATE_ROWS:(g + 1) * NSA_GATE_ROWS]
        for h in range(hpg):
            r = slice(g * qrows + h * dh, g * qrows + (h + 1) * dh)
            o_ref[r, :] = (oc_ref[r, :] + _gate_row(gates_g, h, 1) * o_s[:, hs(h)]
                           + _gate_row(gates_g, h, 2) * o_w[g][:, hs(h)])


def _nsa_attn(qt, ngt, gate_b, kcmp, vcmpt, ks, vst, kw, vwt):
    bsz, g_n, seq, dh = kw.shape
    nr = kcmp.shape[2]
    n_sel = seq // NSA_SEL_BLOCK
    n_cmp = (seq - NSA_CMP_BLOCK) // NSA_CMP_STRIDE + 1
    cmp_start = jnp.arange(nr) * NSA_CMP_STRIDE
    sel_start = jnp.arange(n_sel) * NSA_SEL_BLOCK
    ovl = ((cmp_start[None, :] < sel_start[:, None] + NSA_SEL_BLOCK)
           & (cmp_start[None, :] + NSA_CMP_BLOCK > sel_start[:, None])
           & (jnp.arange(nr)[None, :] < n_cmp)).astype(BF16)
    gb = jnp.pad(gate_b.reshape(g_n, NSA_HPG * NSA_BRANCHES), ((0, 0), (0, NSA_GATE_ROWS - NSA_HPG * NSA_BRANCHES)))
    gb = gb.reshape(g_n, NSA_GATE_ROWS, 1)
    qrows = NSA_HPG * dh
    full = lambda a, c: pl.BlockSpec((None, None, a, c), lambda b, g, n: (b, g, 0, 0))
    cols = lambda r, w: pl.BlockSpec((None, r, w), lambda b, g, n: (b, g, n))
    gb_spec = pl.BlockSpec((None, NSA_GATE_ROWS, 1), lambda b, g, n: (g, 0, 0))
    sw = min(SEL_QBLOCKS * NSA_QBLOCK, seq)
    oc, neg = pl.pallas_call(
        _nsa_select_kernel,
        grid=(bsz, g_n, seq // sw),
        in_specs=[cols(qrows, sw), cols(NSA_GATE_ROWS, sw), gb_spec,
                  full(nr, dh), full(dh, nr), _const_spec((n_sel, nr))],
        out_specs=(cols(qrows, sw), pl.BlockSpec((None, None, n_sel, sw), lambda b, g, n: (b, g, 0, n))),
        out_shape=(jax.ShapeDtypeStruct((bsz, g_n * qrows, seq), F32),
                   jax.ShapeDtypeStruct((bsz, g_n, n_sel, seq), F32)),
        compiler_params=_cparams("parallel", "parallel", "parallel"),
        name="nsa_select",
    )(qt, ngt, gb, kcmp, vcmpt, ovl)
    qb = min(ATTN_QBLOCK, seq)
    n_qb = seq // qb
    qcols = lambda r: pl.BlockSpec((None, r, qb), lambda b, n: (b, 0, n))
    whole = lambda a, c: pl.BlockSpec((None, g_n, a, c), lambda b, n: (b, 0, 0, 0))
    return pl.pallas_call(
        _nsa_attn_kernel,
        grid=(bsz, n_qb),
        in_specs=[qcols(g_n * qrows), qcols(g_n * NSA_GATE_ROWS), _const_spec((g_n * NSA_GATE_ROWS, 1)),
                  qcols(g_n * qrows), pl.BlockSpec((None, g_n, n_sel, qb), lambda b, n: (b, 0, 0, n)),
                  whole(seq, LANE), whole(NSA_VROWS, seq), whole(seq, dh), whole(NSA_VROWS, seq)],
        out_specs=qcols(g_n * qrows),
        out_shape=jax.ShapeDtypeStruct((bsz, g_n * qrows, seq), F32),
        scratch_shapes=[pltpu.VMEM((g_n, LANE, NSA_HPG * qb), BF16),
                        pltpu.VMEM((g_n, SEL_KV_TILE, NSA_HPG * qb), F32),
                        pltpu.VMEM((g_n, 1, NSA_HPG * qb), F32),
                        pltpu.VMEM((g_n, min(NSA_WINDOW + qb, seq), NSA_HPG * qb), F32)],
        compiler_params=_cparams("parallel", "parallel"),
        name="nsa_attn",
    )(qt, ngt, gb.reshape(g_n * NSA_GATE_ROWS, 1), oc, neg, ks, vst, kw, vwt)


def _ffn_residual(h, g_in_ref, w1_ref, w2_ref, g_out_ref):
    xn = _rms(h, g_in_ref[...]).astype(BF16)
    hid = w1_ref.shape[1]
    step = 1024
    acc = jnp.zeros(h.shape, F32)
    for j in range(0, hid, step):
        a = jnp.maximum(_dot(xn, w1_ref[:, j:j + step]), 0.0)
        acc = acc + _dot((a * a).astype(BF16), w2_ref[j:j + step, :])
    return h + _rms(acc, g_out_ref[...])


def _even_out_ffn_kernel(h_ref, oa_ref, obt_ref, w_ref, g_ref, g_in_ref, w1_ref, w2_ref, g_out_ref, o_ref):
    m = _dot(oa_ref[...].astype(BF16), w_ref[:GLA_V, :]) + lax.dot_general(
        obt_ref[...].astype(BF16), w_ref[GLA_V:, :], (((0,), (0,)), ((), ())), preferred_element_type=F32)
    h = h_ref[...] + _rms(m, g_ref[...])
    o_ref[...] = _ffn_residual(h, g_in_ref, w1_ref, w2_ref, g_out_ref)


def _even_out_ffn(h, o_a, o_bt, w_out, g, g_in, w1, w2, g_out):
    bsz, seq, d = h.shape
    tm = min(ROW_TILE, seq)
    row = lambda w: pl.BlockSpec((None, tm, w), lambda b, i: (b, i, 0))
    return pl.pallas_call(
        _even_out_ffn_kernel,
        grid=(bsz, seq // tm),
        in_specs=[row(d), row(GLA_V), pl.BlockSpec((None, NSA_Q, tm), lambda b, i: (b, 0, i)),
                  _const_spec(w_out.shape), _const_spec((1, d)),
                  _const_spec((1, d)), _const_spec(w1.shape), _const_spec(w2.shape), _const_spec((1, d))],
        out_specs=row(d),
        out_shape=jax.ShapeDtypeStruct(h.shape, F32),
        compiler_params=_cparams("parallel", "parallel"),
        name="even_out_ffn",
    )(h, o_a, o_bt, w_out.astype(BF16), g.reshape(1, d),
      g_in.reshape(1, d), w1.astype(BF16), w2.astype(BF16), g_out.reshape(1, d))


def _ffn_kernel(h_ref, g_in_ref, w1_ref, w2_ref, g_out_ref, o_ref):
    o_ref[...] = _ffn_residual(h_ref[...], g_in_ref, w1_ref, w2_ref, g_out_ref)


def _ffn(h, g_in, w1, w2, g_out):
    bsz, seq, d = h.shape
    tm = min(ROW_TILE, seq)
    row = pl.BlockSpec((None, tm, d), lambda b, i: (b, i, 0))
    return pl.pallas_call(
        _ffn_kernel,
        grid=(bsz, seq // tm),
        in_specs=[row, _const_spec((1, d)), _const_spec(w1.shape), _const_spec(w2.shape), _const_spec((1, d))],
        out_specs=row,
        out_shape=jax.ShapeDtypeStruct(h.shape, F32),
        compiler_params=_cparams("parallel", "parallel"),
        name="ffn",
    )(h, g_in.reshape(1, d), w1.astype(BF16), w2.astype(BF16), g_out.reshape(1, d))


def _sgu_kernel(h_ref, g_in_ref, w_in_ref, ln_g_ref, ln_b_ref, ws_ref, bs_ref, w_out_ref, g_out_ref, o_ref):
    e, c, gw = SGU_WIDTH, SGU_CHUNK, SGU_WIDTH // SGU_GROUPS
    tm = h_ref.shape[0]
    half = tm // 2
    causal = lax.broadcasted_iota(jnp.int32, (c, c), 0) >= lax.broadcasted_iota(jnp.int32, (c, c), 1)
    w_c = [jnp.where(causal, ws_ref[g], 0.0).astype(BF16) for g in range(SGU_GROUPS)]

    def normed(r0):
        return _rms(h_ref[r0:r0 + half, :], g_in_ref[...]).astype(BF16)

    def gate(xn):
        v = jax.nn.gelu(_dot(xn, w_in_ref[:, e:]))
        mu = jnp.mean(v, axis=-1, keepdims=True)
        var = jnp.mean((v - mu) ** 2, axis=-1, keepdims=True)
        return ((v - mu) * lax.rsqrt(var + NORM_EPS) * ln_g_ref[...] + ln_b_ref[...]).astype(BF16)

    def mix(xn, vn):
        u = jax.nn.gelu(_dot(xn, w_in_ref[:, :e]))
        rows = []
        for r0 in range(0, half, c):
            mixed = [_dot(w_c[g], vn[r0:r0 + c, g * gw:(g + 1) * gw]) for g in range(SGU_GROUPS)]
            rows.append(jnp.concatenate(mixed, axis=1) + bs_ref[...])
        return (u * jnp.concatenate(rows, axis=0)).astype(BF16)

    def finish(r0, y):
        o_ref[r0:r0 + half, :] = h_ref[r0:r0 + half, :] + _rms(_dot(y, w_out_ref[...]), g_out_ref[...])

    xn_a, xn_b = normed(0), normed(half)
    vn_a = gate(xn_a)
    vn_b = gate(xn_b)
    y_a = mix(xn_a, vn_a)
    y_b = mix(xn_b, vn_b)
    finish(0, y_a)
    finish(half, y_b)


def _sgu(h, g_in, w_in, ln_g, ln_b, w_s, b_s, w_out, g_out):
    bsz, seq, d = h.shape
    tm = min(SGU_ROW_TILE, seq)
    e = SGU_WIDTH
    bias = jnp.repeat(b_s.T, e // SGU_GROUPS, axis=1)
    row = pl.BlockSpec((None, tm, d), lambda b, i: (b, i, 0))
    return pl.pallas_call(
        _sgu_kernel,
        grid=(bsz, seq // tm),
        in_specs=[row, _const_spec((1, d)), _const_spec(w_in.shape), _const_spec((1, e)), _const_spec((1, e)),
                  _const_spec(w_s.shape), _const_spec(bias.shape), _const_spec(w_out.shape), _const_spec((1, d))],
        out_specs=row,
        out_shape=jax.ShapeDtypeStruct(h.shape, F32),
        compiler_params=_cparams("parallel", "parallel"),
        name="sgu",
    )(h, g_in.reshape(1, d), w_in.astype(BF16), ln_g.reshape(1, e), ln_b.reshape(1, e), w_s, bias,
      w_out.astype(BF16), g_out.reshape(1, d))


def _even_mixer_outputs(h, g_in, w_in, w_gate, b_gate, gla_norm, gate_b, cmp_pos, cmp_w1, cmp_w2):
    (gq, gk, gv, gvt, gr, glog, kc, vc, ks, kw, qt, vst, vwt, ngt) = _even_proj(h, g_in, w_in, w_gate, b_gate)
    o_a = _gla(gq, gk, gv, gvt, gr, glog, gla_norm)
    kcmp, vcmpt = _nsa_compress(kc, vc, cmp_pos, cmp_w1, cmp_w2)
    return o_a, _nsa_attn(qt, ngt, gate_b, kcmp, vcmpt, ks, vst, kw, vwt)


def kernel(x, norm_g, ffn_w1, ffn_w2, e_w_in, e_w_out, gla_w_gate, gla_b_gate, gla_norm, nsa_gate_b, nsa_cmp_pos, nsa_cmp_w1, nsa_cmp_w2, o_w_in, o_ln_g, o_ln_b, o_w_s, o_b_s, o_w_out):
    h = x
    depth = norm_g.shape[0]
    for layer in range(depth):
        i = layer // 2
        ffn_args = (norm_g[layer, 2], ffn_w1[layer], ffn_w2[layer], norm_g[layer, 3])
        if layer % 2 == 0:
            o_a, o_bt = _even_mixer_outputs(h, norm_g[layer, 0], e_w_in[i], gla_w_gate[i], gla_b_gate[i],
                                            gla_norm[i], nsa_gate_b[i], nsa_cmp_pos[i], nsa_cmp_w1[i], nsa_cmp_w2[i])
            h = _even_out_ffn(h, o_a, o_bt, e_w_out[i], norm_g[layer, 1], *ffn_args)
        else:
            h = _sgu(h, norm_g[layer, 0], o_w_in[i], o_ln_g[i], o_ln_b[i], o_w_s[i], o_b_s[i], o_w_out[i],
                     norm_g[layer, 1])
            h = _ffn(h, *ffn_args)
    return h
```

```python
import functools

import jax
import jax.numpy as jnp
from jax import lax
from jax.experimental import pallas as pl
from jax.experimental.pallas import tpu as pltpu

F32 = jnp.float32
BF16 = jnp.bfloat16

D_MODEL = 1024
GLA_HEADS = 4
GLA_DK = 64
GLA_DV = 128
GLA_GATE_RANK = 16
GLA_GATE_TEMP = 16.0
GLA_CHUNK = 64
NSA_HEADS = 8
NSA_KV_GROUPS = 2
NSA_HPG = NSA_HEADS // NSA_KV_GROUPS
NSA_DH = 64
NSA_CMP_BLOCK = 32
NSA_CMP_STRIDE = 16
NSA_CMP_HIDDEN = 128
NSA_SEL_BLOCK = 64
NSA_TOPN = 16
NSA_WINDOW = 512
NSA_QBLOCK = 128
NSA_BRANCHES = 3
SGU_WIDTH = 2 * D_MODEL
SGU_GROUPS = 8
SGU_CHUNK = 128
FFN_HIDDEN = 4 * D_MODEL
NORM_EPS = 1e-6
NEG_BIG = -1e30
POS_BIG = 1e30

GLA_QK = GLA_HEADS * GLA_DK
GLA_V = GLA_HEADS * GLA_DV
NSA_Q = NSA_HEADS * NSA_DH
NSA_KV = NSA_KV_GROUPS * NSA_DH
NSA_GATE_ROWS = 16
NSA_VROWS = NSA_DH + 16
LOG2E = 1.4426950408889634
LANE = 128

VMEM_LIMIT_BYTES = 56 * 1024 * 1024

ROW_TILE = 512
FFN_ROW_TILE = 1024
SGU_ROW_TILE = 512
GLA_ROW_TILE = 256
GLA_BATCH_PER_STEP = 8
SEL_KV_TILE = 512
ATTN_QBLOCK = 256
SEL_VARIANTS = 8
SEL_QBLOCKS = 4


def _cparams(*sem):
    return pltpu.CompilerParams(dimension_semantics=sem, vmem_limit_bytes=VMEM_LIMIT_BYTES)


def _rms(x, g):
    return x * lax.rsqrt(jnp.mean(x * x, axis=-1, keepdims=True) + NORM_EPS) * g


def _dot(a, b):
    return jnp.dot(a, b, preferred_element_type=F32)


def _dot_nt(a, b):
    return lax.dot_general(a, b, (((1,), (1,)), ((), ())), preferred_element_type=F32)


def _split3(x):
    x1 = x.astype(BF16)
    r1 = x - x1.astype(F32)
    x2 = r1.astype(BF16)
    r2 = r1 - x2.astype(F32)
    return x1, x2, r2.astype(BF16)


def _dot_exact_lhs(a_bf16, b_f32):
    b1, b2, b3 = _split3(b_f32)
    return _dot(a_bf16, b1) + _dot(a_bf16, b2) + _dot(a_bf16, b3)


def _const_spec(shape):
    return pl.BlockSpec(shape, lambda *_: (0,) * len(shape))


def _even_proj_kernel(h_ref, g_ref, wrow_ref, wt_ref, wlr_ref, wgate_ref, bgate_ref,
                      gq_ref, gk_ref, gv_ref, gvt_ref, gr_ref, glog_ref,
                      kc_ref, vc_ref, ks_ref, kw_ref, qt_ref, vst_ref, vwt_ref, ngt_ref):
    tm_all = h_ref.shape[0]
    tm = tm_all // 2
    w1, w2, w3 = _split3(wgate_ref[...])
    lane = lax.broadcasted_iota(jnp.int32, (tm, NSA_DH), 1)
    blk = lax.broadcasted_iota(jnp.int32, (tm, NSA_DH), 0) // NSA_SEL_BLOCK
    ones_row = jnp.where(lax.broadcasted_iota(jnp.int32, (NSA_VROWS - NSA_DH, tm), 0) == 0, 1.0, 0.0)

    def row_major(r0, xn):
        rows = slice(r0, r0 + tm)
        o = 0
        gq_ref[rows, :] = _dot(xn, wrow_ref[:, o:o + GLA_QK]) * (GLA_DK ** -0.5)
        o += GLA_QK
        gk_ref[rows, :] = _dot(xn, wrow_ref[:, o:o + GLA_QK])
        o += GLA_QK
        gv_ref[rows, :] = _dot(xn, wrow_ref[:, o:o + GLA_V]).astype(BF16)
        o += GLA_V
        gr_ref[rows, :] = _dot(xn, wrow_ref[:, o:o + GLA_V])
        o += GLA_V
        for ref in (kc_ref, vc_ref, kw_ref):
            y = _dot(xn, wrow_ref[:, o:o + NSA_KV]).astype(BF16)
            for g in range(NSA_KV_GROUPS):
                ref[g, rows, :] = y[:, g * NSA_DH:(g + 1) * NSA_DH]
            o += NSA_KV
        y = _dot(xn, wrow_ref[:, o:o + NSA_KV])
        onehot = jnp.where(lane == (blk + r0 // NSA_SEL_BLOCK) % (SEL_KV_TILE // NSA_SEL_BLOCK), 1.0, 0.0)
        for g in range(NSA_KV_GROUPS):
            ks_ref[g, rows, :] = jnp.concatenate([y[:, g * NSA_DH:(g + 1) * NSA_DH], onehot], axis=1).astype(BF16)
        lr = _dot(xn, wlr_ref[...])
        l1, l2, l3 = _split3(lr)
        z = (_dot(l1, w1) + _dot(l1, w2) + _dot(l2, w1) + _dot(l2, w2) + _dot(l1, w3) + _dot(l3, w1)
             + bgate_ref[...])
        glog_ref[rows, :] = jax.nn.log_sigmoid(z) * (1.0 / GLA_GATE_TEMP)

    def transposed(r0, xn):
        cols = slice(r0, r0 + tm)
        o = 0
        gvt_ref[:, cols] = _dot_nt(wt_ref[o:o + GLA_V, :], xn).astype(BF16)
        o += GLA_V
        qt_ref[:, cols] = (_dot_nt(wt_ref[o:o + NSA_Q, :], xn) * (NSA_DH ** -0.5 * LOG2E)).astype(BF16)
        o += NSA_Q
        for ref in (vst_ref, vwt_ref):
            y = _dot_nt(wt_ref[o:o + NSA_KV, :], xn)
            for g in range(NSA_KV_GROUPS):
                ref[g, :, cols] = jnp.concatenate([y[g * NSA_DH:(g + 1) * NSA_DH], ones_row], axis=0).astype(BF16)
            o += NSA_KV
        ngt_ref[:, cols] = _dot_nt(wt_ref[o:o + 2 * NSA_GATE_ROWS, :], xn)

    xn_a = _rms(h_ref[0:tm, :], g_ref[...]).astype(BF16)
    xn_b = _rms(h_ref[tm:tm_all, :], g_ref[...]).astype(BF16)
    row_major(0, xn_a)
    transposed(0, xn_a)
    row_major(tm, xn_b)
    transposed(tm, xn_b)


def _even_proj(h, g, w_in, w_gate, b_gate):
    bsz, seq, d = h.shape
    tm = min(ROW_TILE, seq)
    sizes = (GLA_QK, GLA_QK, GLA_V, GLA_GATE_RANK, GLA_V, NSA_Q) + (NSA_KV,) * 6 + (NSA_HEADS * NSA_BRANCHES,)
    offs = [0]
    for s in sizes:
        offs.append(offs[-1] + s)
    col = lambda i: w_in[:, offs[i]:offs[i + 1]]
    gq, gk, gv, glr, gr, nq, kc, vc, ks, vs, kw, vw, ng = [col(i) for i in range(13)]
    wrow = jnp.concatenate([gq, gk, gv, gr, kc, vc, kw, ks], axis=1).astype(BF16)
    ng_g = ng.reshape(d, NSA_KV_GROUPS, NSA_HPG * NSA_BRANCHES)
    ng_g = jnp.pad(ng_g, ((0, 0), (0, 0), (0, NSA_GATE_ROWS - NSA_HPG * NSA_BRANCHES)))
    wt = jnp.concatenate([gv, nq, vs, vw, ng_g.reshape(d, NSA_KV_GROUPS * NSA_GATE_ROWS)], axis=1).T.astype(BF16)
    wlr = jnp.pad(glr, ((0, 0), (0, LANE - GLA_GATE_RANK))).astype(BF16)
    wgate = jnp.pad(w_gate, ((0, LANE - GLA_GATE_RANK), (0, 0)))
    n_t = wt.shape[0]

    row = lambda w: pl.BlockSpec((None, tm, w), lambda b, i: (b, i, 0))
    grp = pl.BlockSpec((None, NSA_KV_GROUPS, tm, NSA_DH), lambda b, i: (b, 0, i, 0))
    tr = lambda w: pl.BlockSpec((None, w, tm), lambda b, i: (b, 0, i))
    grp_ext = pl.BlockSpec((None, NSA_KV_GROUPS, tm, LANE), lambda b, i: (b, 0, i, 0))
    grp_tr = pl.BlockSpec((None, NSA_KV_GROUPS, NSA_VROWS, tm), lambda b, i: (b, 0, 0, i))
    sds = jax.ShapeDtypeStruct
    kv_shape = sds((bsz, NSA_KV_GROUPS, seq, NSA_DH), BF16)
    vt_shape = sds((bsz, NSA_KV_GROUPS, NSA_VROWS, seq), BF16)
    out_shape = (
        sds((bsz, seq, GLA_QK), F32), sds((bsz, seq, GLA_QK), F32), sds((bsz, seq, GLA_V), BF16),
        sds((bsz, GLA_V, seq), BF16), sds((bsz, seq, GLA_V), F32), sds((bsz, seq, GLA_QK), F32),
        kv_shape, kv_shape, sds((bsz, NSA_KV_GROUPS, seq, LANE), BF16), kv_shape,
        sds((bsz, NSA_Q, seq), BF16), vt_shape, vt_shape,
        sds((bsz, NSA_KV_GROUPS * NSA_GATE_ROWS, seq), F32),
    )
    out_specs = (
        row(GLA_QK), row(GLA_QK), row(GLA_V), tr(GLA_V), row(GLA_V), row(GLA_QK),
        grp, grp, grp_ext, grp,
        tr(NSA_Q), grp_tr, grp_tr, tr(NSA_KV_GROUPS * NSA_GATE_ROWS),
    )
    return pl.pallas_call(
        _even_proj_kernel,
        grid=(bsz, seq // tm),
        in_specs=[
            row(d), _const_spec((1, d)), _const_spec(wrow.shape), _const_spec((n_t, d)),
            _const_spec((d, LANE)), _const_spec((LANE, GLA_QK)), _const_spec((1, GLA_QK)),
        ],
        out_specs=out_specs,
        out_shape=out_shape,
        compiler_params=_cparams("parallel", "parallel"),
        name="even_proj",
    )(h, g.reshape(1, d), wrow, wt, wlr, wgate, b_gate.reshape(1, GLA_QK))


def _gla_kernel(q_ref, k_ref, v_ref, vt_ref, r_ref, glog_ref, ng_ref, o_ref, state_ref):
    c = GLA_CHUNK
    ts = q_ref.shape[1]

    @pl.when(pl.program_id(1) == 0)
    def _():
        state_ref[...] = jnp.zeros_like(state_ref)

    c2 = 2 * c
    ri = lax.broadcasted_iota(jnp.int32, (c2, c2), 0)
    ci = lax.broadcasted_iota(jnp.int32, (c2, c2), 1)
    causal = ri >= ci
    tril2 = jnp.where(causal & ((ri < c) | (ci >= c)), 1.0, 0.0).astype(BF16)
    first = lax.broadcasted_iota(jnp.int32, (c2, 1), 0) < c
    nb = q_ref.shape[0]
    state = [[state_ref[e, h] for h in range(GLA_HEADS)] for e in range(nb)]

    for p0 in range(0, ts, c2):
        rows = slice(p0, p0 + c2)
        for e in range(nb):
            b = _dot_exact_lhs(tril2, glog_ref[e, rows, :])
            b_last0, b_last1 = b[c - 1:c, :], b[c2 - 1:c2, :]
            d0, d1 = jnp.exp(b_last0), jnp.exp(b_last1)
            q_t = q_ref[e, rows, :] * jnp.exp(b)
            k = k_ref[e, rows, :]
            k_t = (k * jnp.exp(-b)).astype(BF16)
            k_end = k * jnp.exp(jnp.where(first, b_last0, b_last1) - b)
            qa = (q_t * jnp.where(first, 1.0, d0)).astype(BF16)
            qb = jnp.where(first, 0.0, q_t).astype(BF16)
            ka = jnp.where(first, k_t, 0.0)
            kb = jnp.where(first, 0.0, k_t)
            ke = (k_end * jnp.where(first, d1, 1.0)).astype(BF16)
            dd = d0 * d1
            for h in range(GLA_HEADS):
                ks = slice(h * GLA_DK, (h + 1) * GLA_DK)
                vs = slice(h * GLA_DV, (h + 1) * GLA_DV)
                a = _dot_nt(jnp.concatenate([qa[:, ks], qb[:, ks]], axis=1),
                            jnp.concatenate([ka[:, ks], kb[:, ks]], axis=1))
                a = jnp.where(causal, a, 0.0).astype(BF16)
                o = _dot(a, v_ref[e, rows, vs]) + _dot_nt(qa[:, ks], state[e][h].astype(BF16))
                state[e][h] = state[e][h] * dd[:, ks] + _dot(vt_ref[e, vs, rows], ke[:, ks])
                o = o * lax.rsqrt(jnp.mean(o * o, axis=-1, keepdims=True) + NORM_EPS) * ng_ref[:, vs]
                o_ref[e, rows, vs] = o * jax.nn.silu(r_ref[e, rows, vs])
    for e in range(nb):
        for h in range(GLA_HEADS):
            state_ref[e, h] = state[e][h]


def _gla(gq, gk, gv, gvt, gr, glog, gla_norm):
    bsz, seq, _ = gq.shape
    ts = min(GLA_ROW_TILE, seq)
    nb = GLA_BATCH_PER_STEP if bsz % GLA_BATCH_PER_STEP == 0 else 1
    row = lambda w: pl.BlockSpec((nb, ts, w), lambda b, i: (b, i, 0))
    return pl.pallas_call(
        _gla_kernel,
        grid=(bsz // nb, seq // ts),
        in_specs=[row(GLA_QK), row(GLA_QK), row(GLA_V),
                  pl.BlockSpec((nb, GLA_V, ts), lambda b, i: (b, 0, i)),
                  row(GLA_V), row(GLA_QK), _const_spec((1, GLA_V))],
        out_specs=row(GLA_V),
        out_shape=jax.ShapeDtypeStruct((bsz, seq, GLA_V), F32),
        scratch_shapes=[pltpu.VMEM((nb, GLA_HEADS, GLA_DV, GLA_DK), F32)],
        compiler_params=_cparams("parallel", "arbitrary"),
        name="gla",
    )(gq, gk, gv, gvt, gr, glog, gla_norm.reshape(1, GLA_V))


def _nsa_compress_kernel(kc_ref, vc_ref, pos_ref, w1_ref, w2_ref, w2t_ref, kcmp_ref, vcmpt_ref):
    half = (NSA_CMP_BLOCK // 2) * NSA_DH

    def hidden(x_ref, i):
        x = x_ref[...]
        first = _dot(x, w1_ref[i, :half, :])
        second = _dot(x, w1_ref[i, half:, :])
        nr = first.shape[0]
        nxt = jnp.concatenate([second[1:], second[:1]], axis=0)
        posc = _dot(pos_ref[i], w1_ref[i])[0:1]
        return jax.nn.gelu(first + nxt + posc).astype(BF16)

    kcmp_ref[...] = _dot(hidden(kc_ref, 0), w2_ref[0]).astype(BF16)
    vcmpt_ref[...] = _dot_nt(w2t_ref[...], hidden(vc_ref, 1)).astype(BF16)


def _nsa_compress(kc, vc, cmp_pos, cmp_w1, cmp_w2):
    bsz, g_n, seq, dh = kc.shape
    grp_tok = NSA_CMP_BLOCK // 2
    nr = seq // grp_tok
    feat = grp_tok * dh
    kc_r = kc.reshape(bsz, g_n, nr, feat)
    vc_r = vc.reshape(bsz, g_n, nr, feat)
    pos = jnp.broadcast_to(cmp_pos.reshape(2, 1, NSA_CMP_BLOCK * dh), (2, 8, NSA_CMP_BLOCK * dh)).astype(BF16)
    blk = pl.BlockSpec((None, None, nr, feat), lambda b, g: (b, g, 0, 0))
    return pl.pallas_call(
        _nsa_compress_kernel,
        grid=(bsz, g_n),
        in_specs=[blk, blk, _const_spec(pos.shape), _const_spec(cmp_w1.shape), _const_spec(cmp_w2.shape),
                  _const_spec((dh, NSA_CMP_HIDDEN))],
        out_specs=(pl.BlockSpec((None, None, nr, dh), lambda b, g: (b, g, 0, 0)),
                   pl.BlockSpec((None, None, dh, nr), lambda b, g: (b, g, 0, 0))),
        out_shape=(jax.ShapeDtypeStruct((bsz, g_n, nr, dh), BF16),
                   jax.ShapeDtypeStruct((bsz, g_n, dh, nr), BF16)),
        compiler_params=_cparams("parallel", "parallel"),
        name="nsa_compress",
    )(kc_r, vc_r, pos, cmp_w1.astype(BF16), cmp_w2.astype(BF16), cmp_w2[1].T.astype(BF16))


def _heads_on_lanes(qt_ref, row0=0):
    return jnp.concatenate([qt_ref[row0 + h * NSA_DH:row0 + (h + 1) * NSA_DH, :] for h in range(NSA_HPG)], axis=1)


def _gate_row(gates, h, branch):
    r = h * NSA_BRANCHES + branch
    return gates[r:r + 1, :]


def _nsa_select_kernel(qt_ref, gt_ref, gb_ref, kcmp_ref, vcmpt_ref, ovl_ref, oc_ref, neg_ref):
    dh, hpg = NSA_DH, NSA_HPG
    w = qt_ref.shape[1]
    nr = kcmp_ref.shape[0]
    n_sel = neg_ref.shape[0]
    step = pl.program_id(2)
    hs = lambda h: slice(h * w, (h + 1) * w)

    def select(ns):
        cr = min(nr, pl.cdiv(ns * (nr // n_sel), LANE) * LANE)
        q = _heads_on_lanes(qt_ref)
        t = step * w + lax.broadcasted_iota(jnp.int32, (1, w), 1)

        s_c = _dot(kcmp_ref[:cr, :], q)
        cmp_end = lax.broadcasted_iota(jnp.int32, (cr, 1), 0) * NSA_CMP_STRIDE + (NSA_CMP_BLOCK - 1)
        ok_c = cmp_end <= t
        has_key = t >= NSA_CMP_BLOCK - 1
        p_c = []
        for h in range(hpg):
            s_h = jnp.where(ok_c, s_c[:, hs(h)], NEG_BIG)
            e = jnp.exp2(s_h - jnp.max(s_h, axis=0, keepdims=True))
            p_c.append(e * jnp.where(has_key, 1.0 / jnp.sum(e, axis=0, keepdims=True), 0.0))
        o_c = _dot(vcmpt_ref[:, :cr], jnp.concatenate(p_c, axis=1).astype(BF16))

        p1, p2, _ = _split3(sum(p_c[1:], p_c[0]))
        imp = _dot(ovl_ref[:ns, :cr], p1) + _dot(ovl_ref[:ns, :cr], p2)
        j = lax.broadcasted_iota(jnp.int32, (ns, 1), 0)
        jf = j.astype(F32)
        cur = t // NSA_SEL_BLOCK
        forced = (j == 0) | (j == cur) | (j == cur - 1)
        score = jnp.where(forced, -jnp.inf, jnp.where(j <= cur, imp, NEG_BIG))
        for _ in range(min(NSA_TOPN, ns) - 3):
            top = jnp.max(score, axis=0, keepdims=True)
            first = jnp.min(jnp.where(score == top, jf, float(ns)), axis=0, keepdims=True)
            score = jnp.where(jf == first, -jnp.inf, score)
        neg_ref[:ns, :] = jnp.where((score == -jnp.inf) & (j <= cur), 0.0, NEG_BIG)
        if ns < n_sel:
            neg_ref[ns:, :] = jnp.full((n_sel - ns, w), NEG_BIG, F32)

        gates = jax.nn.sigmoid(gt_ref[...] + gb_ref[...])
        for h in range(hpg):
            oc_ref[h * dh:(h + 1) * dh, :] = _gate_row(gates, h, 0) * o_c[:, hs(h)]

    n_var = max(1, min(SEL_VARIANTS, n_sel // NSA_TOPN))
    blocks_per_step = w // NSA_SEL_BLOCK
    lo = 0
    for v in range(1, n_var + 1):
        ns = n_sel * v // n_var
        hi = ns // blocks_per_step
        pl.when((step >= lo) & (step < hi))(functools.partial(select, ns))
        lo = hi


def _nsa_attn_kernel(qt_ref, gt_ref, gb_ref, oc_ref, neg_ref, ks_ref, vst_ref, kw_ref, vwt_ref, o_ref,
                     qext_ref, s_ref, smax_ref, sw_ref):
    qb, dh, hpg, sb = qt_ref.shape[1], NSA_DH, NSA_HPG, NSA_SEL_BLOCK
    assert NSA_KV_GROUPS == 2
    groups = range(NSA_KV_GROUPS)
    qrows = hpg * dh
    n = pl.program_id(1)
    t0 = n * qb
    seq = kw_ref.shape[1]
    hs = lambda h: slice(h * qb, (h + 1) * qb)
    q = [_heads_on_lanes(qt_ref, g * qrows) for g in groups]
    t = t0 + lax.broadcasted_iota(jnp.int32, (1, qb), 1)

    tk = SEL_KV_TILE
    bpt = tk // sb
    mrows = 16
    for g in groups:
        qext_ref[g, 0:dh, :] = q[g]
        qext_ref[g, dh + mrows:, :] = jnp.zeros((qext_ref.shape[1] - dh - mrows, hpg * qb), BF16)
    kpos_tile = lax.broadcasted_iota(jnp.int32, (tk, 1), 0)

    def scores(g, kt):
        rows = neg_ref[g, pl.ds(pl.multiple_of(kt * bpt, bpt), bpt), :]
        ext = jnp.concatenate([rows, jnp.zeros((mrows - bpt, qb), F32)], axis=0).astype(BF16)
        qext_ref[g, dh:dh + mrows, :] = jnp.concatenate([ext] * hpg, axis=1)
        s = _dot(ks_ref[g, pl.ds(pl.multiple_of(kt * tk, tk), tk), :], qext_ref[g])
        s_ref[g] = s
        smax_ref[g] = jnp.max(s, axis=0, keepdims=True)

    def accumulate(g, kt, carry_g, diagonal):
        m, acc = carry_g
        k0 = pl.multiple_of(kt * tk, tk)
        if diagonal:
            ok = kpos_tile + k0 <= t
            s_d = [jnp.where(ok, s_ref[g, :, hs(h)], NEG_BIG) for h in range(hpg)]
            m_new = jnp.maximum(m, jnp.concatenate([jnp.max(s_h, axis=0, keepdims=True) for s_h in s_d], axis=1))
        else:
            m_new = jnp.maximum(m, smax_ref[g])
        p = []
        for h in range(hpg):
            s_h = s_d[h] if diagonal else s_ref[g, :, hs(h)]
            p.append(jnp.exp2(s_h - m_new[:, hs(h)]).astype(BF16))
        acc = jnp.exp2(m - m_new) * acc + _dot(vst_ref[g, :, pl.ds(k0, tk)], jnp.concatenate(p, axis=1))
        return m_new, acc

    def sel_step(kt, carry):
        scores(0, kt)
        c1 = accumulate(1, kt, carry[1], False)
        scores(1, kt + 1)
        c0 = accumulate(0, kt, carry[0], False)
        return c0, c1

    wk = sw_ref.shape[1]
    start = pl.multiple_of(jnp.maximum(t0 + qb - wk, 0), qb)
    kpos = start + lax.broadcasted_iota(jnp.int32, (wk, 1), 0)
    ok_w = (kpos <= t) & (kpos > t - NSA_WINDOW)

    def window_scores(g):
        sw_ref[g] = _dot(kw_ref[g, pl.ds(start, wk), :], q[g])

    def window_out(g):
        p_w = []
        for h in range(hpg):
            s_h = jnp.where(ok_w, sw_ref[g, :, hs(h)], NEG_BIG)
            p_w.append(jnp.exp2(s_h - jnp.max(s_h, axis=0, keepdims=True)).astype(BF16))
        acc_w = _dot(vwt_ref[g, :, pl.ds(start, wk)], jnp.concatenate(p_w, axis=1))
        return acc_w[:dh] / jnp.maximum(acc_w[dh:dh + 1], 1e-30)

    last = (n + tk // qb) // (tk // qb) - 1
    init = (jnp.full((1, hpg * qb), NEG_BIG, F32), jnp.zeros((NSA_VROWS, hpg * qb), F32))
    window_scores(0)
    scores(1, 0)
    o_w = [window_out(0)]
    window_scores(1)
    carry = lax.fori_loop(0, last, sel_step, (init, init))
    scores(0, last)
    o_w.append(window_out(1))
    carry = (carry[0], accumulate(1, last, carry[1], True))
    carry = (accumulate(0, last, carry[0], True), carry[1])

    gates = jax.nn.sigmoid(gt_ref[...] + gb_ref[...])
    for g in groups:
        acc_s = carry[g][1]
        o_s = acc_s[:dh] / jnp.maximum(acc_s[dh:dh + 1], 1e-30)
        gates_g = gates[g * NSA_GATE_ROWS:(g + 1) * NSA_GATE_ROWS]
        for h in range(hpg):
            r = slice(g * qrows + h * dh, g * qrows + (h + 1) * dh)
            o_ref[r, :] = (oc_ref[r, :] + _gate_row(gates_g, h, 1) * o_s[:, hs(h)]
                           + _gate_row(gates_g, h, 2) * o_w[g][:, hs(h)])


def _nsa_attn(qt, ngt, gate_b, kcmp, vcmpt, ks, vst, kw, vwt):
    bsz, g_n, seq, dh = kw.shape
    nr = kcmp.shape[2]
    n_sel = seq // NSA_SEL_BLOCK
    n_cmp = (seq - NSA_CMP_BLOCK) // NSA_CMP_STRIDE + 1
    cmp_start = jnp.arange(nr) * NSA_CMP_STRIDE
    sel_start = jnp.arange(n_sel) * NSA_SEL_BLOCK
    ovl = ((cmp_start[None, :] < sel_start[:, None] + NSA_SEL_BLOCK)
           & (cmp_start[None, :] + NSA_CMP_BLOCK > sel_start[:, None])
           & (jnp.arange(nr)[None, :] < n_cmp)).astype(BF16)
    gb = jnp.pad(gate_b.reshape(g_n, NSA_HPG * NSA_BRANCHES), ((0, 0), (0, NSA_GATE_ROWS - NSA_HPG * NSA_BRANCHES)))
    gb = gb.reshape(g_n, NSA_GATE_ROWS, 1)
    qrows = NSA_HPG * dh
    full = lambda a, c: pl.BlockSpec((None, None, a, c), lambda b, g, n: (b, g, 0, 0))
    cols = lambda r, w: pl.BlockSpec((None, r, w), lambda b, g, n: (b, g, n))
    gb_spec = pl.BlockSpec((None, NSA_GATE_ROWS, 1), lambda b, g, n: (g, 0, 0))
    sw = min(SEL_QBLOCKS * NSA_QBLOCK, seq)
    oc, neg = pl.pallas_call(
        _nsa_select_kernel,
        grid=(bsz, g_n, seq // sw),
        in_specs=[cols(qrows, sw), cols(NSA_GATE_ROWS, sw), gb_spec,
                  full(nr, dh), full(dh, nr), _const_spec((n_sel, nr))],
        out_specs=(cols(qrows, sw), pl.BlockSpec((None, None, n_sel, sw), lambda b, g, n: (b, g, 0, n))),
        out_shape=(jax.ShapeDtypeStruct((bsz, g_n * qrows, seq), F32),
                   jax.ShapeDtypeStruct((bsz, g_n, n_sel, seq), F32)),
        compiler_params=_cparams("parallel", "parallel", "parallel"),
        name="nsa_select",
    )(qt, ngt, gb, kcmp, vcmpt, ovl)
    qb = min(ATTN_QBLOCK, seq)
    n_qb = seq // qb
    qcols = lambda r: pl.BlockSpec((None, r, qb), lambda b, n: (b, 0, n))
    whole = lambda a, c: pl.BlockSpec((None, g_n, a, c), lambda b, n: (b, 0, 0, 0))
    return pl.pallas_call(
        _nsa_attn_kernel,
        grid=(bsz, n_qb),
        in_specs=[qcols(g_n * qrows), qcols(g_n * NSA_GATE_ROWS), _const_spec((g_n * NSA_GATE_ROWS, 1)),
                  qcols(g_n * qrows), pl.BlockSpec((None, g_n, n_sel, qb), lambda b, n: (b, 0, 0, n)),
                  whole(seq, LANE), whole(NSA_VROWS, seq), whole(seq, dh), whole(NSA_VROWS, seq)],
        out_specs=qcols(g_n * qrows),
        out_shape=jax.ShapeDtypeStruct((bsz, g_n * qrows, seq), F32),
        scratch_shapes=[pltpu.VMEM((g_n, LANE, NSA_HPG * qb), BF16),
                        pltpu.VMEM((g_n, SEL_KV_TILE, NSA_HPG * qb), F32),
                        pltpu.VMEM((g_n, 1, NSA_HPG * qb), F32),
                        pltpu.VMEM((g_n, min(NSA_WINDOW + qb, seq), NSA_HPG * qb), F32)],
        compiler_params=_cparams("parallel", "parallel"),
        name="nsa_attn",
    )(qt, ngt, gb.reshape(g_n * NSA_GATE_ROWS, 1), oc, neg, ks, vst, kw, vwt)


def _ffn_residual(h, g_in_ref, w1_ref, w2_ref, g_out_ref):
    xn = _rms(h, g_in_ref[...]).astype(BF16)
    hid = w1_ref.shape[1]
    step = 1024
    acc = jnp.zeros(h.shape, F32)
    for j in range(0, hid, step):
        a = jnp.maximum(_dot(xn, w1_ref[:, j:j + step]), 0.0)
        acc = acc + _dot((a * a).astype(BF16), w2_ref[j:j + step, :])
    return h + _rms(acc, g_out_ref[...])


def _even_out_ffn_kernel(h_ref, oa_ref, obt_ref, w_ref, g_ref, g_in_ref, w1_ref, w2_ref, g_out_ref, o_ref):
    m = _dot(oa_ref[...].astype(BF16), w_ref[:GLA_V, :]) + lax.dot_general(
        obt_ref[...].astype(BF16), w_ref[GLA_V:, :], (((0,), (0,)), ((), ())), preferred_element_type=F32)
    h = h_ref[...] + _rms(m, g_ref[...])
    o_ref[...] = _ffn_residual(h, g_in_ref, w1_ref, w2_ref, g_out_ref)


def _even_out_ffn(h, o_a, o_bt, w_out, g, g_in, w1, w2, g_out):
    bsz, seq, d = h.shape
    tm = min(ROW_TILE, seq)
    row = lambda w: pl.BlockSpec((None, tm, w), lambda b, i: (b, i, 0))
    return pl.pallas_call(
        _even_out_ffn_kernel,
        grid=(bsz, seq // tm),
        in_specs=[row(d), row(GLA_V), pl.BlockSpec((None, NSA_Q, tm), lambda b, i: (b, 0, i)),
                  _const_spec(w_out.shape), _const_spec((1, d)),
                  _const_spec((1, d)), _const_spec(w1.shape), _const_spec(w2.shape), _const_spec((1, d))],
        out_specs=row(d),
        out_shape=jax.ShapeDtypeStruct(h.shape, F32),
        compiler_params=_cparams("parallel", "parallel"),
        name="even_out_ffn",
    )(h, o_a, o_bt, w_out.astype(BF16), g.reshape(1, d),
      g_in.reshape(1, d), w1.astype(BF16), w2.astype(BF16), g_out.reshape(1, d))


def _ffn_kernel(h_ref, g_in_ref, w1_ref, w2_ref, g_out_ref, o_ref):
    o_ref[...] = _ffn_residual(h_ref[...], g_in_ref, w1_ref, w2_ref, g_out_ref)


def _ffn(h, g_in, w1, w2, g_out):
    bsz, seq, d = h.shape
    tm = min(FFN_ROW_TILE, seq)
    row = pl.BlockSpec((None, tm, d), lambda b, i: (b, i, 0))
    resident = lambda shape: pl.BlockSpec(shape, lambda *_: (0,) * len(shape), pipeline_mode=pl.Buffered(1))
    return pl.pallas_call(
        _ffn_kernel,
        grid=(bsz, seq // tm),
        in_specs=[row, _const_spec((1, d)), resident(w1.shape), resident(w2.shape), _const_spec((1, d))],
        out_specs=row,
        out_shape=jax.ShapeDtypeStruct(h.shape, F32),
        compiler_params=_cparams("parallel", "parallel"),
        name="ffn",
    )(h, g_in.reshape(1, d), w1.astype(BF16), w2.astype(BF16), g_out.reshape(1, d))


def _sgu_kernel(h_ref, g_in_ref, w_in_ref, ln_g_ref, ln_b_ref, ws_ref, bs_ref, w_out_ref, g_out_ref, o_ref):
    e, c, gw = SGU_WIDTH, SGU_CHUNK, SGU_WIDTH // SGU_GROUPS
    tm = h_ref.shape[0]
    half = tm // 2
    causal = lax.broadcasted_iota(jnp.int32, (c, c), 0) >= lax.broadcasted_iota(jnp.int32, (c, c), 1)
    w_c = [jnp.where(causal, ws_ref[g], 0.0).astype(BF16) for g in range(SGU_GROUPS)]

    def normed(r0):
        return _rms(h_ref[r0:r0 + half, :], g_in_ref[...]).astype(BF16)

    def gate(xn):
        v = jax.nn.gelu(_dot(xn, w_in_ref[:, e:]))
        mu = jnp.mean(v, axis=-1, keepdims=True)
        var = jnp.mean((v - mu) ** 2, axis=-1, keepdims=True)
        return ((v - mu) * lax.rsqrt(var + NORM_EPS) * ln_g_ref[...] + ln_b_ref[...]).astype(BF16)

    def mix(xn, vn):
        u = jax.nn.gelu(_dot(xn, w_in_ref[:, :e]))
        rows = []
        for r0 in range(0, half, c):
            mixed = [_dot(w_c[g], vn[r0:r0 + c, g * gw:(g + 1) * gw]) for g in range(SGU_GROUPS)]
            rows.append(jnp.concatenate(mixed, axis=1) + bs_ref[...])
        return (u * jnp.concatenate(rows, axis=0)).astype(BF16)

    def finish(r0, y):
        o_ref[r0:r0 + half, :] = h_ref[r0:r0 + half, :] + _rms(_dot(y, w_out_ref[...]), g_out_ref[...])

    xn_a, xn_b = normed(0), normed(half)
    vn_a = gate(xn_a)
    vn_b = gate(xn_b)
    y_a = mix(xn_a, vn_a)
    y_b = mix(xn_b, vn_b)
    finish(0, y_a)
    finish(half, y_b)


def _sgu(h, g_in, w_in, ln_g, ln_b, w_s, b_s, w_out, g_out):
    bsz, seq, d = h.shape
    tm = min(SGU_ROW_TILE, seq)
    e = SGU_WIDTH
    bias = jnp.repeat(b_s.T, e // SGU_GROUPS, axis=1)
    row = pl.BlockSpec((None, tm, d), lambda b, i: (b, i, 0))
    return pl.pallas_call(
        _sgu_kernel,
        grid=(bsz, seq // tm),
        in_specs=[row, _const_spec((1, d)), _const_spec(w_in.shape), _const_spec((1, e)), _const_spec((1, e)),
                  _const_spec(w_s.shape), _const_spec(bias.shape), _const_spec(w_out.shape), _const_spec((1, d))],
        out_specs=row,
        out_shape=jax.ShapeDtypeStruct(h.shape, F32),
        compiler_params=_cparams("parallel", "parallel"),
        name="sgu",
    )(h, g_in.reshape(1, d), w_in.astype(BF16), ln_g.reshape(1, e), ln_b.reshape(1, e), w_s, bias,
      w_out.astype(BF16), g_out.reshape(1, d))


def _even_mixer_outputs(h, g_in, w_in, w_gate, b_gate, gla_norm, gate_b, cmp_pos, cmp_w1, cmp_w2):
    (gq, gk, gv, gvt, gr, glog, kc, vc, ks, kw, qt, vst, vwt, ngt) = _even_proj(h, g_in, w_in, w_gate, b_gate)
    o_a = _gla(gq, gk, gv, gvt, gr, glog, gla_norm)
    kcmp, vcmpt = _nsa_compress(kc, vc, cmp_pos, cmp_w1, cmp_w2)
    return o_a, _nsa_attn(qt, ngt, gate_b, kcmp, vcmpt, ks, vst, kw, vwt)


def kernel(x, norm_g, ffn_w1, ffn_w2, e_w_in, e_w_out, gla_w_gate, gla_b_gate, gla_norm, nsa_gate_b, nsa_cmp_pos, nsa_cmp_w1, nsa_cmp_w2, o_w_in, o_ln_g, o_ln_b, o_w_s, o_b_s, o_w_out):
    h = x
    depth = norm_g.shape[0]
    for layer in range(depth):
        i = layer // 2
        ffn_args = (norm_g[layer, 2], ffn_w1[layer], ffn_w2[layer], norm_g[layer, 3])
        if layer % 2 == 0:
            o_a, o_bt = _even_mixer_outputs(h, norm_g[layer, 0], e_w_in[i], gla_w_gate[i], gla_b_gate[i],
                                            gla_norm[i], nsa_gate_b[i], nsa_cmp_pos[i], nsa_cmp_w1[i], nsa_cmp_w2[i])
            h = _even_out_ffn(h, o_a, o_bt, e_w_out[i], norm_g[layer, 1], *ffn_args)
        else:
            h = _sgu(h, norm_g[layer, 0], o_w_in[i], o_ln_g[i], o_ln_b[i], o_w_s[i], o_b_s[i], o_w_out[i],
                     norm_g[layer, 1])
            h = _ffn(h, *ffn_args)
    return h
```
